```python
import math
import jax, jax.numpy as jnp
from jax import lax
import numpy as np

D_MODEL = 2048
BATCH = 4
SEQ = 2048
DEPTH = 4

D_MIX = D_MODEL
D_MLSTM = D_MIX // 2
D_MOBA = D_MIX - D_MLSTM
MLSTM_HEADS = 4
MLSTM_HD = D_MLSTM // MLSTM_HEADS
MLSTM_CHUNK = 64
MLSTM_CONV = 4
MOBA_HEADS = 8
MOBA_HD = D_MOBA // MOBA_HEADS
MOBA_BLOCK = 256
MOBA_TOPK = 3
MOBA_QCHUNK = 32
N_BUCKETS = 32
MAX_DISTANCE = 2048
D_FF = 5632
FFN_CONV = 3
EPS = 1e-6
D_IN = 4 * D_MLSTM + 2 * MLSTM_HEADS + 3 * D_MOBA
MIX_SPLITS = (D_MLSTM, 2 * D_MLSTM, 3 * D_MLSTM, 4 * D_MLSTM,
              4 * D_MLSTM + MLSTM_HEADS, 4 * D_MLSTM + 2 * MLSTM_HEADS,
              4 * D_MLSTM + 2 * MLSTM_HEADS + D_MOBA, 4 * D_MLSTM + 2 * MLSTM_HEADS + 2 * D_MOBA)

kernel_name = "hybrid_mlstm_moba_convffn"


def rms_norm(x, g):
    xf = x.astype(jnp.float32)
    y = xf * lax.rsqrt(jnp.mean(xf * xf, axis=-1, keepdims=True) + EPS)
    return (y * g.astype(jnp.float32)).astype(x.dtype)


def causal_dwconv(x, w, b):
    K = w.shape[0]
    S = x.shape[1]
    xp = jnp.pad(x, ((0, 0), (K - 1, 0), (0, 0)))
    y = b
    for j in range(K):
        y = y + w[j] * xp[:, j:j + S]
    return y


def split_heads(t, n_heads):
    B, S, _ = t.shape
    return t.reshape(B, S, n_heads, -1).transpose(0, 2, 1, 3)


def merge_heads(t):
    B, H, S, dh = t.shape
    return t.transpose(0, 2, 1, 3).reshape(B, S, H * dh)


def t5_bucket(dist):
    max_exact = N_BUCKETS // 2
    d = jnp.maximum(dist, 0)
    ratio = jnp.maximum(d, max_exact).astype(jnp.float32) / max_exact
    large = max_exact + (jnp.log(ratio) / math.log(MAX_DISTANCE / max_exact)
                         * (N_BUCKETS - max_exact)).astype(jnp.int32)
    return jnp.where(d < max_exact, d, jnp.minimum(large, N_BUCKETS - 1))


def mlstm_chunkwise(q, k, v, log_i, log_f):
    B, H, S, dh = q.shape
    L = MLSTM_CHUNK
    nc = S // L
    f32 = jnp.float32

    def chunks(t):
        return jnp.moveaxis(t.astype(f32).reshape(B, H, nc, L, *t.shape[3:]), 2, 0)

    qc = chunks(q) * (dh ** -0.5)
    kc, vc, lic, lfc = chunks(k), chunks(v), chunks(log_i), chunks(log_f)
    causal = jnp.tril(jnp.ones((L, L), dtype=bool))

    def step(carry, inp):
        C, n, m = carry
        qt, kt, vt, li, lf = inp
        b = jnp.cumsum(lf, axis=-1)
        D = b[..., :, None] - b[..., None, :] + li[..., None, :]
        D = jnp.where(causal, D, -jnp.inf)
        inter = b + m[..., None]
        m_t = jnp.maximum(inter, jnp.max(D, axis=-1))
        w_inter = jnp.exp(inter - m_t)
        s = jnp.einsum('bhtd,bhsd->bhts', qt, kt) * jnp.exp(D - m_t[..., None])
        num = (w_inter[..., None] * jnp.einsum('bhtd,bhde->bhte', qt, C)
               + jnp.einsum('bhts,bhse->bhte', s, vt))
        den = w_inter * jnp.einsum('bhtd,bhd->bht', qt, n) + jnp.sum(s, axis=-1)
        h = num / jnp.maximum(jnp.abs(den), jnp.exp(-m_t))[..., None]
        bL = b[..., -1]
        g = bL[..., None] - b + li
        m_new = jnp.maximum(bL + m, jnp.max(g, axis=-1))
        a = jnp.exp(bL + m - m_new)
        wk = jnp.exp(g - m_new[..., None])
        C = a[..., None, None] * C + jnp.einsum('bhs,bhsd,bhse->bhde', wk, kt, vt)
        n = a[..., None] * n + jnp.einsum('bhs,bhsd->bhd', wk, kt)
        return (C, n, m_new), h

    init = (jnp.zeros((B, H, dh, dh), f32), jnp.zeros((B, H, dh), f32), jnp.zeros((B, H), f32))
    _, h = lax.scan(step, init, (qc, kc, vc, lic, lfc))
    h = jnp.moveaxis(h, 0, 2).reshape(B, H, S, dh)
    return h.astype(q.dtype)


def moba_attention(q, k, v, rel_bias):
    B, H, S, dh = q.shape
    f32 = jnp.float32
    nb = -(-S // MOBA_BLOCK)
    sp = nb * MOBA_BLOCK
    pad = ((0, 0), (0, 0), (0, sp - S), (0, 0))
    q, k, v = jnp.pad(q, pad), jnp.pad(k, pad), jnp.pad(v, pad)
    kb = k.reshape(B, H, nb, MOBA_BLOCK, dh)
    vb = v.reshape(B, H, nb, MOBA_BLOCK, dh)
    k_mean = jnp.mean(kb.astype(f32), axis=3)
    q_blk = jnp.arange(sp) // MOBA_BLOCK
    gate = jnp.einsum('bhsd,bhnd->bhsn', q.astype(f32), k_mean)
    past = jnp.arange(nb)[None, :] < q_blk[:, None]
    gate = jnp.where(past, gate, -jnp.inf)
    n_sel = min(MOBA_TOPK, nb)
    _, sel = lax.top_k(gate, n_sel)
    valid = sel < q_blk[:, None]
    nq = sp // MOBA_QCHUNK

    def to_chunks(t):
        return jnp.moveaxis(t.reshape(B, H, nq, MOBA_QCHUNK, *t.shape[3:]), 2, 0)

    bias_t = rel_bias.astype(f32).T
    b_idx = jnp.arange(B)[:, None, None, None]
    h_idx = jnp.arange(H)[None, :, None, None]
    offs = jnp.arange(MOBA_BLOCK)
    scale = dh ** -0.5
    n_cat = n_sel * MOBA_BLOCK

    def one_chunk(args):
        qc, selc, validc, c = args
        q_pos = c * MOBA_QCHUNK + jnp.arange(MOBA_QCHUNK)
        k_sel = kb[b_idx, h_idx, selc]
        v_sel = vb[b_idx, h_idx, selc]
        s_sel = jnp.einsum('bhqd,bhqnkd->bhqnk', qc, k_sel, preferred_element_type=f32) * scale
        k_pos = selc[..., None] * MOBA_BLOCK + offs
        s_sel = s_sel + bias_t[h_idx[..., None], t5_bucket(q_pos[:, None, None] - k_pos)]
        s_sel = jnp.where(validc[..., None], s_sel, -jnp.inf).reshape(B, H, MOBA_QCHUNK, n_cat)
        own = (c * MOBA_QCHUNK) // MOBA_BLOCK
        k_own = lax.dynamic_index_in_dim(kb, own, axis=2, keepdims=False)
        v_own = lax.dynamic_index_in_dim(vb, own, axis=2, keepdims=False)
        rel = q_pos[:, None] - (own * MOBA_BLOCK + offs)[None, :]
        s_own = (jnp.einsum('bhqd,bhkd->bhqk', qc, k_own, preferred_element_type=f32) * scale
                 + bias_t[:, t5_bucket(rel)])
        s_own = jnp.where(rel >= 0, s_own, -jnp.inf)
        p = jax.nn.softmax(jnp.concatenate([s_sel, s_own], axis=-1), axis=-1).astype(v.dtype)
        p_sel = p[..., :n_cat].reshape(B, H, MOBA_QCHUNK, n_sel, MOBA_BLOCK)
        out = (jnp.einsum('bhqnk,bhqnkd->bhqd', p_sel, v_sel, preferred_element_type=f32)
               + jnp.einsum('bhqk,bhkd->bhqd', p[..., n_cat:], v_own, preferred_element_type=f32))
        return out.astype(q.dtype)

    out = lax.map(one_chunk, (to_chunks(q), to_chunks(sel), to_chunks(valid), jnp.arange(nq)))
    out = jnp.moveaxis(out, 0, 2).reshape(B, H, sp, dh)
    return out[:, :, :S]


def setup_inputs(seed: int = 0) -> dict:
    key = jax.random.key(seed)
    ks = jax.random.split(key, 18)

    def nrm(k, shape, scale):
        return jax.random.normal(k, shape, jnp.float32) * scale

    x = nrm(ks[0], (BATCH, SEQ, D_MODEL), 1.0)
    norm_mix = 1.0 + nrm(ks[1], (DEPTH, D_MODEL), 0.02)
    w_in = nrm(ks[2], (DEPTH, D_MODEL, D_IN), D_MODEL ** -0.5)
    gate_bias = jnp.concatenate(
        [nrm(ks[3], (DEPTH, MLSTM_HEADS), 0.1),
         jnp.linspace(3.0, 6.0, MLSTM_HEADS)[None, :] + nrm(ks[4], (DEPTH, MLSTM_HEADS), 0.1)], axis=-1)
    conv_qk_w = nrm(ks[5], (DEPTH, MLSTM_CONV, 2 * D_MLSTM), MLSTM_CONV ** -0.5)
    conv_qk_b = nrm(ks[6], (DEPTH, 2 * D_MLSTM), 0.02)
    mlstm_norm = 1.0 + nrm(ks[7], (DEPTH, MLSTM_HEADS, MLSTM_HD), 0.02)
    qk_norm = 1.0 + nrm(ks[8], (DEPTH, 2, MOBA_HD), 0.02)
    rel_bias = nrm(ks[9], (N_BUCKETS, MOBA_HEADS), 0.5)
    w_out = nrm(ks[10], (DEPTH, D_MIX, D_MODEL), D_MIX ** -0.5)
    norm_ffn = 1.0 + nrm(ks[11], (DEPTH, D_MODEL), 0.02)
    w_up = nrm(ks[12], (DEPTH, D_MODEL, 2 * D_FF), D_MODEL ** -0.5)
    conv_ffn_w = nrm(ks[13], (DEPTH, FFN_CONV, 2 * D_FF), FFN_CONV ** -0.5)
    conv_ffn_b = nrm(ks[14], (DEPTH, 2 * D_FF), 0.02)
    w_down = nrm(ks[15], (DEPTH, D_FF, D_MODEL), D_FF ** -0.5)
    return {"x": x, "norm_mix": norm_mix, "w_in": w_in, "gate_bias": gate_bias,
            "conv_qk_w": conv_qk_w, "conv_qk_b": conv_qk_b, "mlstm_norm": mlstm_norm,
            "qk_norm": qk_norm, "rel_bias": rel_bias, "w_out": w_out, "norm_ffn": norm_ffn,
            "w_up": w_up, "conv_ffn_w": conv_ffn_w, "conv_ffn_b": conv_ffn_b, "w_down": w_down}


def reference(x, norm_mix, w_in, gate_bias, conv_qk_w, conv_qk_b, mlstm_norm, qk_norm, rel_bias,
              w_out, norm_ffn, w_up, conv_ffn_w, conv_ffn_b, w_down):
    H = MLSTM_HEADS
    for l in range(DEPTH):
        h = rms_norm(x, norm_mix[l])
        z = jnp.einsum('bsd,de->bse', h, w_in[l])
        mq, mk, mv, mo, gi, gf, aq, ak, av = jnp.split(z, MIX_SPLITS, axis=-1)
        qk = jax.nn.silu(causal_dwconv(jnp.concatenate([mq, mk], axis=-1), conv_qk_w[l], conv_qk_b[l]))
        mq, mk = qk[..., :D_MLSTM], qk[..., D_MLSTM:]
        gates = jnp.concatenate([gi, gf], axis=-1).astype(jnp.float32) + gate_bias[l]
        log_i = gates[..., :H].transpose(0, 2, 1)
        log_f = jax.nn.log_sigmoid(gates[..., H:]).transpose(0, 2, 1)
        hm = mlstm_chunkwise(split_heads(mq, H), split_heads(mk, H), split_heads(mv, H), log_i, log_f)
        hm = rms_norm(hm, mlstm_norm[l][:, None, :])
        hm = merge_heads(hm) * jax.nn.sigmoid(mo)
        aqh = rms_norm(split_heads(aq, MOBA_HEADS), qk_norm[l, 0])
        akh = rms_norm(split_heads(ak, MOBA_HEADS), qk_norm[l, 1])
        ha = merge_heads(moba_attention(aqh, akh, split_heads(av, MOBA_HEADS), rel_bias))
        x = x + jnp.einsum('bse,ed->bsd', jnp.concatenate([hm, ha], axis=-1), w_out[l])
        h = rms_norm(x, norm_ffn[l])
        u = causal_dwconv(jnp.einsum('bsd,df->bsf', h, w_up[l]), conv_ffn_w[l], conv_ffn_b[l])
        g, val = u[..., :D_FF], u[..., D_FF:]
        x = x + jnp.einsum('bsf,fd->bsd', jax.nn.silu(g) * val, w_down[l])
    return x
```

```python
import functools
import math

import numpy as np
import jax
import jax.numpy as jnp
from jax import lax
from jax.experimental import pallas as pl
from jax.experimental.pallas import tpu as pltpu

F32 = jnp.float32
BF16 = jnp.bfloat16

EPS = 1e-6
MLSTM_HEADS = 4
MLSTM_CONV = 4
MLSTM_CHUNK = 256
MOBA_HEADS = 8
MOBA_BLOCK = 256
MOBA_TOPK = 3
N_BUCKETS = 32
MAX_DISTANCE = 2048
FFN_CONV = 3

LANES = 128
VMEM_LIMIT = 56 * 1024 * 1024


def _t5_thresholds():
    max_exact = N_BUCKETS // 2
    d = np.arange(0, MAX_DISTANCE, dtype=np.float64)
    ratio = np.maximum(d, max_exact) / max_exact
    large = max_exact + (np.log(ratio) / math.log(MAX_DISTANCE / max_exact)
                         * (N_BUCKETS - max_exact)).astype(np.int64)
    bucket = np.where(d < max_exact, d.astype(np.int64), np.minimum(large, N_BUCKETS - 1))
    assert np.all(np.diff(bucket) >= 0)
    thr = [int(np.argmax(bucket >= k)) if np.any(bucket >= k) else MAX_DISTANCE for k in range(N_BUCKETS)]
    return bucket, thr


_T5_BUCKET, _T5_THR = _t5_thresholds()


def _sigmoid(x):
    return 1.0 / (1.0 + jnp.exp(-x))


def _split3(x):
    hi = x.astype(BF16)
    r1 = x - hi.astype(F32)
    mid = r1.astype(BF16)
    lo = (r1 - mid.astype(F32)).astype(BF16)
    return hi, mid, lo


def _dot(a, b):
    return jnp.dot(a, b, preferred_element_type=F32)


def _dot_nt(a, b):
    return lax.dot_general(a, b, (((1,), (1,)), ((), ())), preferred_element_type=F32)


def _dot_tn(a, b):
    return lax.dot_general(a, b, (((0,), (0,)), ((), ())), preferred_element_type=F32)


def _params(*sem):
    return pltpu.CompilerParams(dimension_semantics=sem, vmem_limit_bytes=VMEM_LIMIT)


def _rms(x, g):
    return x * lax.rsqrt(jnp.mean(x * x, axis=-1, keepdims=True) + EPS) * g


def _norm_kernel(x_ref, g_ref, h_ref):
    h_ref[...] = _rms(x_ref[...], g_ref[...]).astype(BF16)


def _norm_gates_kernel(x_ref, g_ref, wg_ref, h_ref, gates_ref):
    h = _rms(x_ref[...], g_ref[...]).astype(BF16)
    h_ref[...] = h
    gates_ref[...] = _dot(h, wg_ref[...])


def _norm(x2, g, wg=None, tm=512):
    T, D = x2.shape
    x_spec = pl.BlockSpec((tm, D), lambda i: (i, 0))
    g_spec = pl.BlockSpec((1, D), lambda i: (0, 0))
    h_spec = pl.BlockSpec((tm, D), lambda i: (i, 0))
    if wg is None:
        return pl.pallas_call(
            _norm_kernel, grid=(T // tm,), in_specs=[x_spec, g_spec], out_specs=h_spec,
            out_shape=jax.ShapeDtypeStruct((T, D), BF16), compiler_params=_params("parallel"),
            name="rmsnorm")(x2, g.reshape(1, D))
    return pl.pallas_call(
        _norm_gates_kernel, grid=(T // tm,),
        in_specs=[x_spec, g_spec, pl.BlockSpec((D, LANES), lambda i: (0, 0))],
        out_specs=[h_spec, pl.BlockSpec((tm, LANES), lambda i: (i, 0))],
        out_shape=[jax.ShapeDtypeStruct((T, D), BF16), jax.ShapeDtypeStruct((T, LANES), F32)],
        compiler_params=_params("parallel"), name="rmsnorm_gates")(x2, g.reshape(1, D), wg)


def _proj_kernel(h_ref, w_ref, o_ref, wbf_ref):
    @pl.when(pl.program_id(1) == 0)
    def _():
        wbf_ref[...] = w_ref[...].astype(BF16)

    o_ref[...] = _dot(h_ref[...], wbf_ref[...]).astype(o_ref.dtype)


def _proj(h, w, layer, col0, ncols, tm=512, tn=1024, out_dtype=F32):
    T, K = h.shape
    assert col0 % tn == 0 and ncols % tn == 0 and T % tm == 0
    c0 = col0 // tn
    return pl.pallas_call(
        _proj_kernel, grid=(ncols // tn, T // tm),
        in_specs=[pl.BlockSpec((tm, K), lambda n, m: (m, 0)),
                  pl.BlockSpec((None, K, tn), lambda n, m: (layer, 0, c0 + n))],
        out_specs=pl.BlockSpec((tm, tn), lambda n, m: (m, n)),
        out_shape=jax.ShapeDtypeStruct((T, ncols), out_dtype),
        scratch_shapes=[pltpu.VMEM((K, tn), BF16)],
        compiler_params=_params("parallel", "arbitrary"), name="in_proj")(h, w)


def _out_proj_kernel(hm_ref, ha_ref, w_ref, x_ref, o_ref, wbf_ref):
    @pl.when(pl.program_id(1) == 0)
    def _():
        wbf_ref[...] = w_ref[...].astype(BF16)

    km = hm_ref.shape[1]
    o_ref[...] = x_ref[...] + (_dot(hm_ref[...], wbf_ref[:km, :]) + _dot(ha_ref[...], wbf_ref[km:, :]))


def _out_proj(hm, ha, w, layer, x2, tm=512, tn=1024):
    T, km = hm.shape
    ka = ha.shape[1]
    N = x2.shape[1]
    return pl.pallas_call(
        _out_proj_kernel, grid=(N // tn, T // tm),
        in_specs=[pl.BlockSpec((tm, km), lambda n, m: (m, 0)),
                  pl.BlockSpec((tm, ka), lambda n, m: (m, 0)),
                  pl.BlockSpec((None, km + ka, tn), lambda n, m: (layer, 0, n)),
                  pl.BlockSpec((tm, tn), lambda n, m: (m, n))],
        out_specs=pl.BlockSpec((tm, tn), lambda n, m: (m, n)),
        out_shape=jax.ShapeDtypeStruct((T, N), F32),
        scratch_shapes=[pltpu.VMEM((km + ka, tn), BF16)],
        compiler_params=_params("parallel", "arbitrary"), name="out_proj")(hm, ha, w, x2)


def _down_proj_kernel(a_ref, w_ref, x_ref, o_ref, wbf_ref):
    @pl.when(pl.program_id(1) == 0)
    def _():
        wbf_ref[...] = w_ref[...].astype(BF16)

    o_ref[...] = x_ref[...] + _dot(a_ref[...], wbf_ref[...])


def _down_proj(a, w, layer, x2, tm=512, tn=512):
    T, K = a.shape
    N = x2.shape[1]
    return pl.pallas_call(
        _down_proj_kernel, grid=(N // tn, T // tm),
        in_specs=[pl.BlockSpec((tm, K), lambda n, m: (m, 0)),
                  pl.BlockSpec((None, K, tn), lambda n, m: (layer, 0, n)),
                  pl.BlockSpec((tm, tn), lambda n, m: (m, n))],
        out_specs=pl.BlockSpec((tm, tn), lambda n, m: (m, n)),
        out_shape=jax.ShapeDtypeStruct((T, N), F32),
        scratch_shapes=[pltpu.VMEM((K, tn), BF16)],
        compiler_params=_params("parallel", "arbitrary"), name="down_proj")(a, w, x2)


def _up_proj_kernel(h_ref, wg_ref, wv_ref, cwg_ref, cwv_ref, cbg_ref, cbv_ref, a_ref,
                    wgbf_ref, wvbf_ref, carry_g_ref, carry_v_ref, *, tiles_per_seq):
    m = pl.program_id(1)

    @pl.when(m == 0)
    def _():
        wgbf_ref[...] = wg_ref[...].astype(BF16)
        wvbf_ref[...] = wv_ref[...].astype(BF16)

    @pl.when(m % tiles_per_seq == 0)
    def _():
        carry_g_ref[...] = jnp.zeros_like(carry_g_ref)
        carry_v_ref[...] = jnp.zeros_like(carry_v_ref)

    h = h_ref[...]
    tm = h.shape[0]
    row = lax.broadcasted_iota(jnp.int32, (tm, 1), 0)

    def conv(u, cw_ref, cb_ref, carry_ref):
        tail = carry_ref[...]
        p1 = jnp.where(row == 0, tail[7:8, :], pltpu.roll(u, 1, 0))
        p2 = jnp.where(row == 0, tail[6:7, :], jnp.where(row == 1, tail[7:8, :], pltpu.roll(u, 2, 0)))
        carry_ref[...] = u[tm - 8:, :]
        cw = cw_ref[...]
        return ((cb_ref[...] + cw[0:1, :] * p2) + cw[1:2, :] * p1) + cw[2:3, :] * u

    yg = conv(_dot(h, wgbf_ref[...]), cwg_ref, cbg_ref, carry_g_ref)
    yv = conv(_dot(h, wvbf_ref[...]), cwv_ref, cbv_ref, carry_v_ref)
    a_ref[...] = (yg * _sigmoid(yg) * yv).astype(BF16)


def _up_proj(h, w, layer, cw, cb, seq_len, tm=512, tf=512):
    T, K = h.shape
    d_ff = w.shape[2] // 2
    nf = d_ff // tf
    assert d_ff % tf == 0 and seq_len % tm == 0
    kern = functools.partial(_up_proj_kernel, tiles_per_seq=seq_len // tm)
    return pl.pallas_call(
        kern, grid=(nf, T // tm),
        in_specs=[pl.BlockSpec((tm, K), lambda n, m: (m, 0)),
                  pl.BlockSpec((None, K, tf), lambda n, m: (layer, 0, n)),
                  pl.BlockSpec((None, K, tf), lambda n, m: (layer, 0, nf + n)),
                  pl.BlockSpec((FFN_CONV, tf), lambda n, m: (0, n)),
                  pl.BlockSpec((FFN_CONV, tf), lambda n, m: (0, nf + n)),
                  pl.BlockSpec((1, tf), lambda n, m: (0, n)),
                  pl.BlockSpec((1, tf), lambda n, m: (0, nf + n))],
        out_specs=pl.BlockSpec((tm, tf), lambda n, m: (m, n)),
        out_shape=jax.ShapeDtypeStruct((T, d_ff), BF16),
        scratch_shapes=[pltpu.VMEM((K, tf), BF16), pltpu.VMEM((K, tf), BF16),
                        pltpu.VMEM((8, tf), F32), pltpu.VMEM((8, tf), F32)],
        compiler_params=_params("parallel", "arbitrary"), name="up_proj")(
            h, w, w, cw, cw, cb.reshape(1, -1), cb.reshape(1, -1))


def _mlstm_kernel(q_ref, k_ref, v_ref, o_ref, g_ref, gb_ref, cwq_ref, cwk_ref, cbq_ref, cbk_ref, nw_ref,
                  out_ref, pad_ref, qs_ref, ks_ref, c_ref, n_ref, m_ref, *, chunk):
    head = pl.program_id(1)
    S, dh = q_ref.shape[1], q_ref.shape[2]
    L = chunk
    nc = S // L
    halo = 8

    def conv_silu(x_ref, cw_ref, cb_ref, dst_ref, scale):
        pad_ref[0:halo, :] = jnp.zeros((halo, dh), F32)
        pad_ref[halo:halo + S, :] = x_ref[0]
        cw = cw_ref[...]
        cb = cb_ref[...]
        for c in range(nc):
            y = cb
            for j in range(MLSTM_CONV):
                off = halo + c * L - (MLSTM_CONV - 1) + j
                y = y + cw[j:j + 1, :] * pad_ref[off:off + L, :]
            dst_ref[c * L:(c + 1) * L, :] = (y * _sigmoid(y) * scale).astype(BF16)

    conv_silu(q_ref, cwq_ref, cbq_ref, qs_ref, dh ** -0.5)
    conv_silu(k_ref, cwk_ref, cbk_ref, ks_ref, 1.0)

    c_ref[...] = jnp.zeros_like(c_ref)
    n_ref[...] = jnp.zeros_like(n_ref)
    m_ref[...] = jnp.zeros_like(m_ref)

    rr = lax.broadcasted_iota(jnp.int32, (L, L), 0)
    cc = lax.broadcasted_iota(jnp.int32, (L, L), 1)
    tri = rr >= cc
    eye = rr == cc
    tri_bf = jnp.where(tri, 1.0, 0.0).astype(BF16)
    lane = lax.broadcasted_iota(jnp.int32, (L, LANES), 1)

    def lane_pick(x, idx):
        return jnp.sum(jnp.where(lane == idx, x, 0.0), axis=1, keepdims=True)

    def body(c, carry):
        t0 = pl.multiple_of(c * L, L)
        G = g_ref[0, pl.ds(t0, L), :] + gb_ref[...]
        li = lane_pick(G, head)
        LF = jnp.minimum(G, 0.0) - jnp.log1p(jnp.exp(-jnp.abs(G)))
        hi, mid, lo = _split3(LF)
        b = lane_pick(_dot(tri_bf, hi) + _dot(tri_bf, mid) + _dot(tri_bf, lo), MLSTM_HEADS + head)

        m_prev = m_ref[...]
        r_row = jnp.sum(jnp.where(eye, li - b, 0.0), axis=0, keepdims=True)
        Dm = jnp.where(tri, b + r_row, -jnp.inf)
        inter = b + m_prev
        m_t = jnp.maximum(inter, jnp.max(Dm, axis=1, keepdims=True))
        w_inter = jnp.exp(inter - m_t)

        qc = qs_ref[pl.ds(t0, L), :]
        kc = ks_ref[pl.ds(t0, L), :]
        vc = v_ref[0, pl.ds(t0, L), :]
        s = _dot_nt(qc, kc) * jnp.exp(Dm - m_t)
        num = w_inter * _dot(qc, c_ref[...].astype(BF16)) + _dot(s.astype(BF16), vc.astype(BF16))
        qn = jnp.sum(qc.astype(F32) * n_ref[...], axis=1, keepdims=True)
        den = w_inter * qn + jnp.sum(s, axis=1, keepdims=True)
        hraw = num * (1.0 / jnp.maximum(jnp.abs(den), jnp.exp(-m_t)))

        y = _rms(hraw, nw_ref[...])
        out_ref[0, pl.ds(t0, L), :] = (y * _sigmoid(o_ref[0, pl.ds(t0, L), :])).astype(BF16)

        bL = b[L - 1:L, :]
        gcol = bL - b + li
        m_new = jnp.maximum(bL + m_prev, jnp.max(gcol, axis=0, keepdims=True))
        a = jnp.exp(bL + m_prev - m_new)
        wk = jnp.exp(gcol - m_new)
        c_ref[...] = a * c_ref[...] + _dot_tn(kc, (wk * vc).astype(BF16))
        n_ref[...] = a * n_ref[...] + jnp.sum(wk * kc.astype(F32), axis=0, keepdims=True)
        m_ref[...] = m_new
        return carry

    lax.fori_loop(0, nc, body, 0)


def _mlstm(zA, gates, gate_bias, conv_w, conv_b, norm_w, chunk=MLSTM_CHUNK):
    B, S, four_dm = zA.shape
    dm = four_dm // 4
    H = MLSTM_HEADS
    dh = dm // H
    kern = functools.partial(_mlstm_kernel, chunk=chunk)

    def col(base):
        return pl.BlockSpec((1, S, dh), lambda b, h: (b, 0, base + h))

    return pl.pallas_call(
        kern, grid=(B, H),
        in_specs=[col(0), col(H), col(2 * H), col(3 * H),
                  pl.BlockSpec((1, S, LANES), lambda b, h: (b, 0, 0)),
                  pl.BlockSpec((1, LANES), lambda b, h: (0, 0)),
                  pl.BlockSpec((MLSTM_CONV, dh), lambda b, h: (0, h)),
                  pl.BlockSpec((MLSTM_CONV, dh), lambda b, h: (0, H + h)),
                  pl.BlockSpec((1, dh), lambda b, h: (0, h)),
                  pl.BlockSpec((1, dh), lambda b, h: (0, H + h)),
                  pl.BlockSpec((None, 1, dh), lambda b, h: (h, 0, 0))],
        out_specs=pl.BlockSpec((1, S, dh), lambda b, h: (b, 0, h)),
        out_shape=jax.ShapeDtypeStruct((B, S, dm), BF16),
        scratch_shapes=[pltpu.VMEM((S + 8, dh), F32), pltpu.VMEM((S, dh), BF16), pltpu.VMEM((S, dh), BF16),
                        pltpu.VMEM((dh, dh), F32), pltpu.VMEM((1, dh), F32), pltpu.VMEM((1, 1), F32)],
        compiler_params=_params("parallel", "parallel"), name="mlstm")(
            zA, zA, zA, zA, gates, gate_bias, conv_w, conv_w, conv_b.reshape(1, -1), conv_b.reshape(1, -1),
            norm_w.reshape(H, 1, dh))


def _moba_kernel(rb_ref, q_ref, k_ref, v_ref, qkn_ref, out_ref,
                 bias_ref, sc_ref, qn_ref, kn_ref, vb_ref, gate_ref):
    head = pl.program_id(0)
    S, dh = q_ref.shape[1], q_ref.shape[2]
    BLK = MOBA_BLOCK
    NB = S // BLK
    scale = dh ** -0.5

    rr = lax.broadcasted_iota(jnp.int32, (BLK, BLK), 0)
    cc = lax.broadcasted_iota(jnp.int32, (BLK, BLK), 1)
    causal_f = jnp.where(rr >= cc, 1.0, 0.0)

    @pl.when(pl.program_id(1) == 0)
    def _build_bias():
        for delta in range(NB):
            d = (rr - cc) + delta * BLK
            lo, hi = max(delta * BLK - (BLK - 1), 0), delta * BLK + BLK - 1
            tile = jnp.full((BLK, BLK), rb_ref[int(_T5_BUCKET[lo]), head], F32)
            for kb in range(int(_T5_BUCKET[lo]) + 1, int(_T5_BUCKET[hi]) + 1):
                tile = jnp.where(d >= _T5_THR[kb], rb_ref[kb, head], tile)
            bias_ref[delta] = tile

    qn = _rms(q_ref[0], qkn_ref[0:1, :])
    kn = _rms(k_ref[0], qkn_ref[1:2, :])
    qn_ref[...] = qn.astype(BF16).reshape(NB, BLK, dh)
    kn_ref[...] = kn.astype(BF16).reshape(NB, BLK, dh)
    vb_ref[...] = v_ref[0].astype(BF16).reshape(NB, BLK, dh)

    kmean = jnp.mean(kn.reshape(NB, BLK, dh), axis=1)
    kmean = jnp.concatenate([kmean, jnp.zeros((LANES - NB, dh), F32)], axis=0)
    q_hi = qn.astype(BF16)
    q_lo = (qn - q_hi.astype(F32)).astype(BF16)
    m_hi = kmean.astype(BF16)
    m_lo = (kmean - m_hi.astype(F32)).astype(BF16)
    gate = _dot_nt(q_hi, m_hi) + (_dot_nt(q_hi, m_lo) + _dot_nt(q_lo, m_hi))
    gate_ref[...] = gate.reshape(NB, BLK, LANES)

    lane = lax.broadcasted_iota(jnp.int32, (BLK, LANES), 1)

    def qblock(i, carry):
        g = gate_ref[i]
        cnt = jnp.zeros((BLK, LANES), F32)
        for jp in range(NB - 1):
            gj = g[:, jp:jp + 1]
            beats = (gj > g) | ((gj == g) & (lane > jp))
            cnt = cnt + jnp.where(beats & (jp < i), 1.0, 0.0)
        sel = jnp.where((cnt < MOBA_TOPK) & (lane < i), 1.0, 0.0)

        qb = qn_ref[i]

        def score(j, mx):
            selcol = jnp.max(jnp.where(lane == j, sel, 0.0), axis=1, keepdims=True)
            keep = jnp.where(j < i, selcol, causal_f)
            s = _dot_nt(qb, kn_ref[j]) * scale + bias_ref[i - j]
            s = jnp.where(keep > 0.5, s, -jnp.inf)
            sc_ref[j] = s
            return jnp.maximum(mx, jnp.max(s, axis=1, keepdims=True))

        mx = lax.fori_loop(0, i + 1, score, jnp.full((BLK, 1), -jnp.inf, F32))

        def accum(j, carry2):
            acc, l = carry2
            p = jnp.exp(sc_ref[j] - mx)
            return acc + _dot(p.astype(BF16), vb_ref[j]), l + jnp.sum(p, axis=1, keepdims=True)

        acc, l = lax.fori_loop(0, i + 1, accum, (jnp.zeros((BLK, dh), F32), jnp.zeros((BLK, 1), F32)))
        out_ref[0, pl.ds(pl.multiple_of(i * BLK, BLK), BLK), :] = (acc * (1.0 / l)).astype(BF16)
        return carry

    lax.fori_loop(0, NB, qblock, 0)


def _moba(zB, qk_norm_l, rel_bias):
    B, S, three_da = zB.shape
    da = three_da // 3
    H = MOBA_HEADS
    dh = da // H
    NB = S // MOBA_BLOCK
    BLK = MOBA_BLOCK

    def col(base):
        return pl.BlockSpec((1, S, dh), lambda h, b: (b, 0, base + h))

    return pl.pallas_call(
        _moba_kernel, grid=(H, B),
        in_specs=[pl.BlockSpec(memory_space=pltpu.SMEM),
                  col(0), col(H), col(2 * H),
                  pl.BlockSpec((2, dh), lambda h, b: (0, 0))],
        out_specs=pl.BlockSpec((1, S, dh), lambda h, b: (b, 0, h)),
        out_shape=jax.ShapeDtypeStruct((B, S, da), BF16),
        scratch_shapes=[pltpu.VMEM((NB, BLK, BLK), F32), pltpu.VMEM((NB, BLK, BLK), F32),
                        pltpu.VMEM((NB, BLK, dh), BF16), pltpu.VMEM((NB, BLK, dh), BF16),
                        pltpu.VMEM((NB, BLK, dh), BF16), pltpu.VMEM((NB, BLK, LANES), F32)],
        compiler_params=_params("parallel", "arbitrary"), name="moba")(rel_bias, zB, zB, zB, qk_norm_l)


def kernel(x, norm_mix, w_in, gate_bias, conv_qk_w, conv_qk_b, mlstm_norm, qk_norm, rel_bias, w_out, norm_ffn,
           w_up, conv_ffn_w, conv_ffn_b, w_down):
    B, S, D = x.shape
    depth = w_in.shape[0]
    H = MLSTM_HEADS
    dm = mlstm_norm.shape[1] * mlstm_norm.shape[2]
    da = w_out.shape[1] - dm
    n_gate = 2 * H
    g0 = 4 * dm
    assert w_in.shape[2] == g0 + n_gate + 3 * da

    w_gate = jnp.pad(w_in[:, :, g0:g0 + n_gate], ((0, 0), (0, 0), (0, LANES - n_gate))).astype(BF16)
    w_moba = w_in[:, :, g0 + n_gate:]
    gb = jnp.pad(gate_bias, ((0, 0), (0, LANES - n_gate)))

    x2 = x.reshape(B * S, D)
    for l in range(depth):
        h, gates = _norm(x2, norm_mix[l], w_gate[l])
        zA = _proj(h, w_in, l, 0, g0)
        zB = _proj(h, w_moba, l, 0, 3 * da)
        hm = _mlstm(zA.reshape(B, S, g0), gates.reshape(B, S, LANES), gb[l:l + 1], conv_qk_w[l], conv_qk_b[l],
                    mlstm_norm[l])
        ha = _moba(zB.reshape(B, S, 3 * da), qk_norm[l], rel_bias)
        x2 = _out_proj(hm.reshape(B * S, dm), ha.reshape(B * S, da), w_out, l, x2)
        h2 = _norm(x2, norm_ffn[l])
        a = _up_proj(h2, w_up, l, conv_ffn_w[l], conv_ffn_b[l], S)
        x2 = _down_proj(a, w_down, l, x2)
    return x2.reshape(B, S, D)
```

```python
import functools
import math

import numpy as np
import jax
import jax.numpy as jnp
from jax import lax
from jax.experimental import pallas as pl
from jax.experimental.pallas import tpu as pltpu

F32 = jnp.float32
BF16 = jnp.bfloat16

EPS = 1e-6
MLSTM_HEADS = 4
MLSTM_CONV = 4
MLSTM_CHUNK = 256
MOBA_HEADS = 8
MOBA_BLOCK = 256
MOBA_TOPK = 3
MASKED = -1e30
N_BUCKETS = 32
MAX_DISTANCE = 2048
FFN_CONV = 3
HALO = 8

LANES = 128
VMEM_LIMIT = 56 * 1024 * 1024


def _t5_thresholds():
    max_exact = N_BUCKETS // 2
    d = np.arange(0, MAX_DISTANCE, dtype=np.float64)
    ratio = np.maximum(d, max_exact) / max_exact
    large = max_exact + (np.log(ratio) / math.log(MAX_DISTANCE / max_exact)
                         * (N_BUCKETS - max_exact)).astype(np.int64)
    bucket = np.where(d < max_exact, d.astype(np.int64), np.minimum(large, N_BUCKETS - 1))
    assert np.all(np.diff(bucket) >= 0)
    thr = [int(np.argmax(bucket >= k)) if np.any(bucket >= k) else MAX_DISTANCE for k in range(N_BUCKETS)]
    return bucket, thr


_T5_BUCKET, _T5_THR = _t5_thresholds()


def _sigmoid(x):
    return 1.0 / (1.0 + jnp.exp(-x))


def _split3(x):
    hi = x.astype(BF16)
    r1 = x - hi.astype(F32)
    mid = r1.astype(BF16)
    lo = (r1 - mid.astype(F32)).astype(BF16)
    return hi, mid, lo


def _dot(a, b):
    return jnp.dot(a, b, preferred_element_type=F32)


def _dot_nt(a, b):
    return lax.dot_general(a, b, (((1,), (1,)), ((), ())), preferred_element_type=F32)


def _dot_tn(a, b):
    return lax.dot_general(a, b, (((0,), (0,)), ((), ())), preferred_element_type=F32)


def _params(*sem, flags=None):
    return pltpu.CompilerParams(dimension_semantics=sem, vmem_limit_bytes=VMEM_LIMIT, flags=flags)


def _rms(x, g):
    return x * lax.rsqrt(jnp.mean(x * x, axis=-1, keepdims=True) + EPS) * g


def _norm_kernel(x_ref, g_ref, h_ref):
    h_ref[...] = _rms(x_ref[...], g_ref[...]).astype(BF16)


def _norm_gates_kernel(x_ref, g_ref, wg_ref, h_ref, gates_ref):
    h = _rms(x_ref[...], g_ref[...]).astype(BF16)
    h_ref[...] = h
    gates_ref[...] = _dot(h, wg_ref[...])


def _norm(x2, g, wg=None, tm=512):
    T, D = x2.shape
    x_spec = pl.BlockSpec((tm, D), lambda i: (i, 0))
    g_spec = pl.BlockSpec((1, D), lambda i: (0, 0))
    h_spec = pl.BlockSpec((tm, D), lambda i: (i, 0))
    if wg is None:
        return pl.pallas_call(
            _norm_kernel, grid=(T // tm,), in_specs=[x_spec, g_spec], out_specs=h_spec,
            out_shape=jax.ShapeDtypeStruct((T, D), BF16), compiler_params=_params("parallel"),
            name="rmsnorm")(x2, g.reshape(1, D))
    return pl.pallas_call(
        _norm_gates_kernel, grid=(T // tm,),
        in_specs=[x_spec, g_spec, pl.BlockSpec((D, LANES), lambda i: (0, 0))],
        out_specs=[h_spec, pl.BlockSpec((tm, LANES), lambda i: (i, 0))],
        out_shape=[jax.ShapeDtypeStruct((T, D), BF16), jax.ShapeDtypeStruct((T, LANES), F32)],
        compiler_params=_params("parallel"), name="rmsnorm_gates")(x2, g.reshape(1, D), wg)


def _proj_kernel(h_ref, w_ref, o_ref, wbf_ref):
    @pl.when(pl.program_id(1) == 0)
    def _():
        wbf_ref[...] = w_ref[...].astype(BF16)

    o_ref[...] = _dot(h_ref[...], wbf_ref[...]).astype(o_ref.dtype)


def _proj_shifted_kernel(h_ref, w_ref, wnext_ref, o_ref, wbf_ref, *, shift):
    @pl.when(pl.program_id(1) == 0)
    def _():
        wide = jnp.concatenate([w_ref[...], wnext_ref[...]], axis=1)
        wbf_ref[...] = wide[:, shift:shift + w_ref.shape[1]].astype(BF16)

    o_ref[...] = _dot(h_ref[...], wbf_ref[...]).astype(o_ref.dtype)


def _proj(h, w, layer, col0, ncols, tm=1024, tn=1024, out_dtype=F32):
    T, K = h.shape
    shift = col0 % LANES
    base = col0 - shift
    assert base % tn == 0 and ncols % tn == 0 and T % tm == 0
    c0 = base // tn
    h_spec = pl.BlockSpec((tm, K), lambda n, m: (m, 0))
    w_spec = pl.BlockSpec((None, K, tn), lambda n, m: (layer, 0, c0 + n))
    common = dict(
        grid=(ncols // tn, T // tm),
        out_specs=pl.BlockSpec((tm, tn), lambda n, m: (m, n)),
        out_shape=jax.ShapeDtypeStruct((T, ncols), out_dtype),
        scratch_shapes=[pltpu.VMEM((K, tn), BF16)],
        compiler_params=_params("parallel", "arbitrary"), name="in_proj")
    if shift == 0:
        return pl.pallas_call(_proj_kernel, in_specs=[h_spec, w_spec], **common)(h, w)
    lanes_per_tile = tn // LANES
    wnext_spec = pl.BlockSpec((None, K, LANES), lambda n, m: (layer, 0, (c0 + n + 1) * lanes_per_tile))
    return pl.pallas_call(functools.partial(_proj_shifted_kernel, shift=shift),
                          in_specs=[h_spec, w_spec, wnext_spec], **common)(h, w, w)


def _out_proj_kernel(hm_ref, ha_ref, w_ref, x_ref, o_ref, wbf_ref):
    @pl.when(pl.program_id(1) == 0)
    def _():
        wbf_ref[...] = w_ref[...].astype(BF16)

    km = hm_ref.shape[1]
    o_ref[...] = x_ref[...] + (_dot(hm_ref[...], wbf_ref[:km, :]) + _dot(ha_ref[...], wbf_ref[km:, :]))


def _out_proj(hm, ha, w, layer, x2, tm=1024, tn=1024):
    T, km = hm.shape
    ka = ha.shape[1]
    N = x2.shape[1]
    return pl.pallas_call(
        _out_proj_kernel, grid=(N // tn, T // tm),
        in_specs=[pl.BlockSpec((tm, km), lambda n, m: (m, 0)),
                  pl.BlockSpec((tm, ka), lambda n, m: (m, 0)),
                  pl.BlockSpec((None, km + ka, tn), lambda n, m: (layer, 0, n)),
                  pl.BlockSpec((tm, tn), lambda n, m: (m, n))],
        out_specs=pl.BlockSpec((tm, tn), lambda n, m: (m, n)),
        out_shape=jax.ShapeDtypeStruct((T, N), F32),
        scratch_shapes=[pltpu.VMEM((km + ka, tn), BF16)],
        compiler_params=_params("parallel", "arbitrary"), name="out_proj")(hm, ha, w, x2)


def _down_proj_kernel(a_ref, w_ref, x_ref, o_ref, wbf_ref):
    @pl.when(pl.program_id(1) == 0)
    def _():
        wbf_ref[...] = w_ref[...].astype(BF16)

    o_ref[...] = x_ref[...] + _dot(a_ref[...], wbf_ref[...])


def _down_proj(a, w, layer, x2, tm=512, tn=512):
    T, K = a.shape
    N = x2.shape[1]
    return pl.pallas_call(
        _down_proj_kernel, grid=(N // tn, T // tm),
        in_specs=[pl.BlockSpec((tm, K), lambda n, m: (m, 0)),
                  pl.BlockSpec((None, K, tn), lambda n, m: (layer, 0, n)),
                  pl.BlockSpec((tm, tn), lambda n, m: (m, n))],
        out_specs=pl.BlockSpec((tm, tn), lambda n, m: (m, n)),
        out_shape=jax.ShapeDtypeStruct((T, N), F32),
        scratch_shapes=[pltpu.VMEM((K, tn), BF16)],
        compiler_params=_params("parallel", "arbitrary"), name="down_proj")(a, w, x2)


def _up_proj_kernel(h_ref, wg_ref, wv_ref, cwg_ref, cwv_ref, cbg_ref, cbv_ref, a_ref,
                    wgbf_ref, wvbf_ref, carry_g_ref, carry_v_ref, *, tiles_per_seq):
    m = pl.program_id(1)

    @pl.when(m == 0)
    def _():
        wgbf_ref[...] = wg_ref[...].astype(BF16)
        wvbf_ref[...] = wv_ref[...].astype(BF16)

    @pl.when(m % tiles_per_seq == 0)
    def _():
        carry_g_ref[...] = jnp.zeros_like(carry_g_ref)
        carry_v_ref[...] = jnp.zeros_like(carry_v_ref)

    h = h_ref[...]
    tm = h.shape[0]
    row = lax.broadcasted_iota(jnp.int32, (tm, 1), 0)

    def conv(u, cw_ref, cb_ref, carry_ref):
        tail = carry_ref[...]
        p1 = jnp.where(row == 0, tail[HALO - 1:HALO, :], pltpu.roll(u, 1, 0))
        p2 = jnp.where(row == 0, tail[HALO - 2:HALO - 1, :],
                       jnp.where(row == 1, tail[HALO - 1:HALO, :], pltpu.roll(u, 2, 0)))
        carry_ref[...] = u[tm - HALO:, :]
        cw = cw_ref[...]
        return ((cb_ref[...] + cw[0:1, :] * p2) + cw[1:2, :] * p1) + cw[2:3, :] * u

    yg = conv(_dot(h, wgbf_ref[...]), cwg_ref, cbg_ref, carry_g_ref)
    yv = conv(_dot(h, wvbf_ref[...]), cwv_ref, cbv_ref, carry_v_ref)
    a_ref[...] = (yg * _sigmoid(yg) * yv).astype(BF16)


def _up_proj(h, w, layer, cw, cb, seq_len, tm=512, tf=512):
    T, K = h.shape
    d_ff = w.shape[2] // 2
    nf = d_ff // tf
    assert d_ff % tf == 0 and seq_len % tm == 0
    kern = functools.partial(_up_proj_kernel, tiles_per_seq=seq_len // tm)
    return pl.pallas_call(
        kern, grid=(nf, T // tm),
        in_specs=[pl.BlockSpec((tm, K), lambda n, m: (m, 0)),
                  pl.BlockSpec((None, K, tf), lambda n, m: (layer, 0, n)),
                  pl.BlockSpec((None, K, tf), lambda n, m: (layer, 0, nf + n)),
                  pl.BlockSpec((FFN_CONV, tf), lambda n, m: (0, n)),
                  pl.BlockSpec((FFN_CONV, tf), lambda n, m: (0, nf + n)),
                  pl.BlockSpec((1, tf), lambda n, m: (0, n)),
                  pl.BlockSpec((1, tf), lambda n, m: (0, nf + n))],
        out_specs=pl.BlockSpec((tm, tf), lambda n, m: (m, n)),
        out_shape=jax.ShapeDtypeStruct((T, d_ff), BF16),
        scratch_shapes=[pltpu.VMEM((K, tf), BF16), pltpu.VMEM((K, tf), BF16),
                        pltpu.VMEM((HALO, tf), F32), pltpu.VMEM((HALO, tf), F32)],
        compiler_params=_params("parallel", "arbitrary"), name="up_proj")(
            h, w, w, cw, cw, cb.reshape(1, -1), cb.reshape(1, -1))


def _mlstm_kernel(q_ref, k_ref, v_ref, o_ref, g_ref, gb_ref, cwq_ref, cwk_ref, cbq_ref, cbk_ref, nw_ref,
                  out_ref, pad_ref, qs_ref, ks_ref, c_ref, n_ref, m_ref, *, chunk):
    head = pl.program_id(1)
    S, dh = q_ref.shape[1], q_ref.shape[2]
    L = chunk
    nc = S // L
    halo = 8

    def conv_silu(x_ref, cw_ref, cb_ref, dst_ref, scale):
        pad_ref[0:halo, :] = jnp.zeros((halo, dh), F32)
        pad_ref[halo:halo + S, :] = x_ref[0]
        cw = cw_ref[...]
        cb = cb_ref[...]
        for c in range(nc):
            y = cb
            for j in range(MLSTM_CONV):
                off = halo + c * L - (MLSTM_CONV - 1) + j
                y = y + cw[j:j + 1, :] * pad_ref[off:off + L, :]
            dst_ref[c * L:(c + 1) * L, :] = (y * _sigmoid(y) * scale).astype(BF16)

    conv_silu(q_ref, cwq_ref, cbq_ref, qs_ref, dh ** -0.5)
    conv_silu(k_ref, cwk_ref, cbk_ref, ks_ref, 1.0)

    c_ref[...] = jnp.zeros_like(c_ref)
    n_ref[...] = jnp.zeros_like(n_ref)
    m_ref[...] = jnp.zeros_like(m_ref)

    rr = lax.broadcasted_iota(jnp.int32, (L, L), 0)
    cc = lax.broadcasted_iota(jnp.int32, (L, L), 1)
    tri = rr >= cc
    eye = rr == cc
    tri_bf = jnp.where(tri, 1.0, 0.0).astype(BF16)
    lane = lax.broadcasted_iota(jnp.int32, (L, LANES), 1)

    def lane_pick(x, idx):
        return jnp.sum(jnp.where(lane == idx, x, 0.0), axis=1, keepdims=True)

    def body(c, carry):
        t0 = pl.multiple_of(c * L, L)
        G = g_ref[0, pl.ds(t0, L), :] + gb_ref[...]
        li = lane_pick(G, head)
        LF = jnp.minimum(G, 0.0) - jnp.log1p(jnp.exp(-jnp.abs(G)))
        hi, mid, lo = _split3(LF)
        b = lane_pick(_dot(tri_bf, hi) + _dot(tri_bf, mid) + _dot(tri_bf, lo), MLSTM_HEADS + head)

        m_prev = m_ref[...]
        r_row = jnp.sum(jnp.where(eye, li - b, 0.0), axis=0, keepdims=True)
        Dm = jnp.where(tri, b + r_row, -jnp.inf)
        inter = b + m_prev
        m_t = jnp.maximum(inter, jnp.max(Dm, axis=1, keepdims=True))
        w_inter = jnp.exp(inter - m_t)

        qc = qs_ref[pl.ds(t0, L), :]
        kc = ks_ref[pl.ds(t0, L), :]
        vc = v_ref[0, pl.ds(t0, L), :]
        s = _dot_nt(qc, kc) * jnp.exp(Dm - m_t)
        num = w_inter * _dot(qc, c_ref[...].astype(BF16)) + _dot(s.astype(BF16), vc.astype(BF16))
        qn = jnp.sum(qc.astype(F32) * n_ref[...], axis=1, keepdims=True)
        den = w_inter * qn + jnp.sum(s, axis=1, keepdims=True)
        hraw = num * (1.0 / jnp.maximum(jnp.abs(den), jnp.exp(-m_t)))

        y = _rms(hraw, nw_ref[...])
        out_ref[0, pl.ds(t0, L), :] = (y * _sigmoid(o_ref[0, pl.ds(t0, L), :])).astype(BF16)

        bL = b[L - 1:L, :]
        gcol = bL - b + li
        m_new = jnp.maximum(bL + m_prev, jnp.max(gcol, axis=0, keepdims=True))
        a = jnp.exp(bL + m_prev - m_new)
        wk = jnp.exp(gcol - m_new)
        c_ref[...] = a * c_ref[...] + _dot_tn(kc, (wk * vc).astype(BF16))
        n_ref[...] = a * n_ref[...] + jnp.sum(wk * kc.astype(F32), axis=0, keepdims=True)
        m_ref[...] = m_new
        return carry

    lax.fori_loop(0, nc, body, 0)


def _mlstm(zA, gates, gate_bias, conv_w, conv_b, norm_w, chunk=MLSTM_CHUNK):
    B, S, four_dm = zA.shape
    dm = four_dm // 4
    H = MLSTM_HEADS
    dh = dm // H
    kern = functools.partial(_mlstm_kernel, chunk=chunk)

    def col(base):
        return pl.BlockSpec((1, S, dh), lambda b, h: (b, 0, base + h))

    return pl.pallas_call(
        kern, grid=(B, H),
        in_specs=[col(0), col(H), col(2 * H), col(3 * H),
                  pl.BlockSpec((1, S, LANES), lambda b, h: (b, 0, 0)),
                  pl.BlockSpec((1, LANES), lambda b, h: (0, 0)),
                  pl.BlockSpec((MLSTM_CONV, dh), lambda b, h: (0, h)),
                  pl.BlockSpec((MLSTM_CONV, dh), lambda b, h: (0, H + h)),
                  pl.BlockSpec((1, dh), lambda b, h: (0, h)),
                  pl.BlockSpec((1, dh), lambda b, h: (0, H + h)),
                  pl.BlockSpec((None, 1, dh), lambda b, h: (h, 0, 0))],
        out_specs=pl.BlockSpec((1, S, dh), lambda b, h: (b, 0, h)),
        out_shape=jax.ShapeDtypeStruct((B, S, dm), BF16),
        scratch_shapes=[pltpu.VMEM((S + 8, dh), F32), pltpu.VMEM((S, dh), BF16), pltpu.VMEM((S, dh), BF16),
                        pltpu.VMEM((dh, dh), F32), pltpu.VMEM((1, dh), F32), pltpu.VMEM((1, 1), F32)],
        compiler_params=_params("parallel", "parallel"), name="mlstm")(
            zA, zA, zA, zA, gates, gate_bias, conv_w, conv_w, conv_b.reshape(1, -1), conv_b.reshape(1, -1),
            norm_w.reshape(H, 1, dh))


def _moba_kernel(rb_ref, q_ref, k_ref, v_ref, qkn_ref, out_ref,
                 bias_ref, sc_ref, qa_ref, ka_ref, va_ref):
    head = pl.program_id(0)
    S, dh = q_ref.shape[1], q_ref.shape[2]
    BLK = MOBA_BLOCK
    NB = S // BLK
    scale = dh ** -0.5

    @pl.when(pl.program_id(1) == 0)
    def _build_bias():
        rr = lax.broadcasted_iota(jnp.int32, (BLK, BLK), 0)
        cc = lax.broadcasted_iota(jnp.int32, (BLK, BLK), 1)
        for delta in range(NB):
            d = (rr - cc) + delta * BLK
            lo, hi = max(delta * BLK - (BLK - 1), 0), delta * BLK + BLK - 1
            tile = jnp.full((BLK, BLK), rb_ref[int(_T5_BUCKET[lo]), head], F32)
            for kb in range(int(_T5_BUCKET[lo]) + 1, int(_T5_BUCKET[hi]) + 1):
                tile = jnp.where(d >= _T5_THR[kb], rb_ref[kb, head], tile)
            if delta == 0:
                tile = jnp.where(d >= 0, tile, -jnp.inf)
            bias_ref[delta] = tile

    qn = _rms(q_ref[0], qkn_ref[0:1, :])
    kn = _rms(k_ref[0], qkn_ref[1:2, :])
    q_hi = qn.astype(BF16)
    qa_ref[:, :, :dh] = q_hi.reshape(NB, BLK, dh)
    lane_k = lax.broadcasted_iota(jnp.int32, (BLK, dh), 1)
    ka_ref[:, :, :dh] = kn.astype(BF16).reshape(NB, BLK, dh)
    for j in range(NB):
        ka_ref[j, :, dh:] = jnp.where(lane_k == j, 1.0, 0.0).astype(BF16)
    va_ref[:, :, :dh] = v_ref[0].astype(BF16).reshape(NB, BLK, dh)
    va_ref[:, :, dh:] = jnp.ones((NB, BLK, dh), BF16)

    kmean = jnp.mean(kn.reshape(NB, BLK, dh), axis=1)
    kmean = jnp.concatenate([kmean, jnp.zeros((16 - NB, dh), F32)], axis=0)
    q_lo = (qn - q_hi.astype(F32)).astype(BF16)
    m_hi = kmean.astype(BF16)
    m_lo = (kmean - m_hi.astype(F32)).astype(BF16)
    gate_t = _dot_nt(m_hi, q_hi) + (_dot_nt(m_lo, q_hi) + _dot_nt(m_hi, q_lo))

    blk = lax.broadcasted_iota(jnp.int32, (NB, BLK), 0)
    for i in range(1, NB):
        g = gate_t[:NB, i * BLK:(i + 1) * BLK]
        cnt = jnp.zeros((NB, BLK), F32)
        for jp in range(i):
            gj = g[jp:jp + 1, :]
            beats = (gj > g) | ((gj == g) & (blk > jp))
            cnt = cnt + jnp.where(beats, 1.0, 0.0)
        mask_t = jnp.where((cnt < MOBA_TOPK) | (blk >= i), 0.0, MASKED)
        mask_t = jnp.concatenate([mask_t, jnp.zeros((dh - NB, BLK), F32)], axis=0)
        qa_ref[i, :, dh:] = mask_t.T.astype(BF16)
    qa_ref[0, :, dh:] = jnp.zeros((BLK, dh), BF16)

    tile = lambda i, j: i * (i + 1) // 2 + j
    mxs = []
    for i in range(NB):
        qa = qa_ref[i]
        mx = None
        for j in range(i + 1):
            s = _dot_nt(qa, ka_ref[j]) * scale + bias_ref[i - j]
            sc_ref[tile(i, j)] = s
            mj = jnp.maximum(s[:, :LANES], s[:, LANES:])
            mx = mj if mx is None else jnp.maximum(mx, mj)
        mxs.append(jnp.max(mx, axis=1, keepdims=True))

    for i in range(NB):
        acc = jnp.zeros((BLK, 2 * dh), F32)
        for j in range(i + 1):
            acc = acc + _dot(jnp.exp(sc_ref[tile(i, j)] - mxs[i]).astype(BF16), va_ref[j])
        out_ref[0, i * BLK:(i + 1) * BLK, :] = (acc[:, :dh] * (1.0 / acc[:, dh:dh + 1])).astype(BF16)


def _moba(zB, qk_norm_l, rel_bias):
    B, S, three_da = zB.shape
    da = three_da // 3
    H = MOBA_HEADS
    dh = da // H
    NB = S // MOBA_BLOCK
    BLK = MOBA_BLOCK

    def col(base):
        return pl.BlockSpec((1, S, dh), lambda h, b: (b, 0, base + h))

    return pl.pallas_call(
        _moba_kernel, grid=(H, B),
        in_specs=[pl.BlockSpec(memory_space=pltpu.SMEM),
                  col(0), col(H), col(2 * H),
                  pl.BlockSpec((2, dh), lambda h, b: (0, 0))],
        out_specs=pl.BlockSpec((1, S, dh), lambda h, b: (b, 0, h)),
        out_shape=jax.ShapeDtypeStruct((B, S, da), BF16),
        scratch_shapes=[pltpu.VMEM((NB, BLK, BLK), F32), pltpu.VMEM((NB * (NB + 1) // 2, BLK, BLK), F32),
                        pltpu.VMEM((NB, BLK, 2 * dh), BF16), pltpu.VMEM((NB, BLK, 2 * dh), BF16),
                        pltpu.VMEM((NB, BLK, 2 * dh), BF16)],
        compiler_params=_params("parallel", "arbitrary"), name="moba")(rel_bias, zB, zB, zB, qk_norm_l)


def kernel(x, norm_mix, w_in, gate_bias, conv_qk_w, conv_qk_b, mlstm_norm, qk_norm, rel_bias, w_out, norm_ffn,
           w_up, conv_ffn_w, conv_ffn_b, w_down):
    B, S, D = x.shape
    depth = w_in.shape[0]
    H = MLSTM_HEADS
    dm = mlstm_norm.shape[1] * mlstm_norm.shape[2]
    da = w_out.shape[1] - dm
    n_gate = 2 * H
    g0 = 4 * dm
    assert w_in.shape[2] == g0 + n_gate + 3 * da

    w_gate = jnp.pad(w_in[:, :, g0:g0 + n_gate], ((0, 0), (0, 0), (0, LANES - n_gate))).astype(BF16)
    gb = jnp.pad(gate_bias, ((0, 0), (0, LANES - n_gate)))

    x2 = x.reshape(B * S, D)
    for l in range(depth):
        h, gates = _norm(x2, norm_mix[l], w_gate[l])
        zA = _proj(h, w_in, l, 0, g0)
        zB = _proj(h, w_in, l, g0 + n_gate, 3 * da)
        hm = _mlstm(zA.reshape(B, S, g0), gates.reshape(B, S, LANES), gb[l:l + 1], conv_qk_w[l], conv_qk_b[l],
                    mlstm_norm[l])
        ha = _moba(zB.reshape(B, S, 3 * da), qk_norm[l], rel_bias)
        x2 = _out_proj(hm.reshape(B * S, dm), ha.reshape(B * S, da), w_out, l, x2)
        h2 = _norm(x2, norm_ffn[l])
        a = _up_proj(h2, w_up, l, conv_ffn_w[l], conv_ffn_b[l], S)
        x2 = _down_proj(a, w_down, l, x2)
    return x2.reshape(B, S, D)
```

```python
import functools
import math

import numpy as np
import jax
import jax.numpy as jnp
from jax import lax
from jax.experimental import pallas as pl
from jax.experimental.pallas import tpu as pltpu

F32 = jnp.float32
BF16 = jnp.bfloat16

EPS = 1e-6
MLSTM_HEADS = 4
MLSTM_CONV = 4
MLSTM_CHUNK = 256
MOBA_HEADS = 8
MOBA_BLOCK = 256
MOBA_TOPK = 3
MASKED = -1e30
N_BUCKETS = 32
MAX_DISTANCE = 2048
FFN_CONV = 3
HALO = 8

LANES = 128
VMEM_LIMIT = 56 * 1024 * 1024


def _t5_thresholds():
    max_exact = N_BUCKETS // 2
    d = np.arange(0, MAX_DISTANCE, dtype=np.float64)
    ratio = np.maximum(d, max_exact) / max_exact
    large = max_exact + (np.log(ratio) / math.log(MAX_DISTANCE / max_exact)
                         * (N_BUCKETS - max_exact)).astype(np.int64)
    bucket = np.where(d < max_exact, d.astype(np.int64), np.minimum(large, N_BUCKETS - 1))
    assert np.all(np.diff(bucket) >= 0)
    thr = [int(np.argmax(bucket >= k)) if np.any(bucket >= k) else MAX_DISTANCE for k in range(N_BUCKETS)]
    return bucket, thr


_T5_BUCKET, _T5_THR = _t5_thresholds()


def _sigmoid(x):
    return 1.0 / (1.0 + jnp.exp(-x))


def _split3(x):
    hi = x.astype(BF16)
    r1 = x - hi.astype(F32)
    mid = r1.astype(BF16)
    lo = (r1 - mid.astype(F32)).astype(BF16)
    return hi, mid, lo


def _dot(a, b):
    return jnp.dot(a, b, preferred_element_type=F32)


def _dot_nt(a, b):
    return lax.dot_general(a, b, (((1,), (1,)), ((), ())), preferred_element_type=F32)


def _dot_tn(a, b):
    return lax.dot_general(a, b, (((0,), (0,)), ((), ())), preferred_element_type=F32)


def _params(*sem, flags=None):
    return pltpu.CompilerParams(dimension_semantics=sem, vmem_limit_bytes=VMEM_LIMIT, flags=flags)


def _rms(x, g):
    return x * lax.rsqrt(jnp.mean(x * x, axis=-1, keepdims=True) + EPS) * g


def _norm_kernel(x_ref, g_ref, h_ref):
    h_ref[...] = _rms(x_ref[...], g_ref[...]).astype(BF16)


def _norm_gates_kernel(x_ref, g_ref, wg_ref, h_ref, gates_ref):
    h = _rms(x_ref[...], g_ref[...]).astype(BF16)
    h_ref[...] = h
    gates_ref[...] = _dot(h, wg_ref[...].astype(BF16))


def _norm(x2, g, w=None, layer=0, gate_col=None, tm=512):
    T, D = x2.shape
    x_spec = pl.BlockSpec((tm, D), lambda i: (i, 0))
    g_spec = pl.BlockSpec((1, D), lambda i: (0, 0))
    h_spec = pl.BlockSpec((tm, D), lambda i: (i, 0))
    if w is None:
        return pl.pallas_call(
            _norm_kernel, grid=(T // tm,), in_specs=[x_spec, g_spec], out_specs=h_spec,
            out_shape=jax.ShapeDtypeStruct((T, D), BF16), compiler_params=_params("parallel"),
            name="rmsnorm")(x2, g.reshape(1, D))
    assert gate_col % LANES == 0
    return pl.pallas_call(
        _norm_gates_kernel, grid=(T // tm,),
        in_specs=[x_spec, g_spec, pl.BlockSpec((None, D, LANES), lambda i: (layer, 0, gate_col // LANES))],
        out_specs=[h_spec, pl.BlockSpec((tm, LANES), lambda i: (i, 0))],
        out_shape=[jax.ShapeDtypeStruct((T, D), BF16), jax.ShapeDtypeStruct((T, LANES), F32)],
        compiler_params=_params("parallel"), name="rmsnorm_gates")(x2, g.reshape(1, D), w)


def _proj_kernel(h_ref, w_ref, o_ref, wbf_ref):
    @pl.when(pl.program_id(1) == 0)
    def _():
        wbf_ref[...] = w_ref[...].astype(BF16)

    o_ref[...] = _dot(h_ref[...], wbf_ref[...]).astype(o_ref.dtype)


def _proj_shifted_kernel(h_ref, w_ref, wnext_ref, o_ref, wbf_ref, *, shift):
    @pl.when(pl.program_id(1) == 0)
    def _():
        wide = jnp.concatenate([w_ref[...], wnext_ref[...]], axis=1)
        wbf_ref[...] = wide[:, shift:shift + w_ref.shape[1]].astype(BF16)

    o_ref[...] = _dot(h_ref[...], wbf_ref[...]).astype(o_ref.dtype)


def _proj(h, w, layer, col0, ncols, tm=1024, tn=1024, out_dtype=F32):
    T, K = h.shape
    shift = col0 % LANES
    base = col0 - shift
    assert base % tn == 0 and ncols % tn == 0 and T % tm == 0
    c0 = base // tn
    h_spec = pl.BlockSpec((tm, K), lambda n, m: (m, 0))
    w_spec = pl.BlockSpec((None, K, tn), lambda n, m: (layer, 0, c0 + n))
    common = dict(
        grid=(ncols // tn, T // tm),
        out_specs=pl.BlockSpec((tm, tn), lambda n, m: (m, n)),
        out_shape=jax.ShapeDtypeStruct((T, ncols), out_dtype),
        scratch_shapes=[pltpu.VMEM((K, tn), BF16)],
        compiler_params=_params("parallel", "arbitrary"), name="in_proj")
    if shift == 0:
        return pl.pallas_call(_proj_kernel, in_specs=[h_spec, w_spec], **common)(h, w)
    lanes_per_tile = tn // LANES
    wnext_spec = pl.BlockSpec((None, K, LANES), lambda n, m: (layer, 0, (c0 + n + 1) * lanes_per_tile))
    return pl.pallas_call(functools.partial(_proj_shifted_kernel, shift=shift),
                          in_specs=[h_spec, w_spec, wnext_spec], **common)(h, w, w)


def _out_proj_kernel(hm_ref, ha_ref, w_ref, x_ref, o_ref, wbf_ref):
    @pl.when(pl.program_id(1) == 0)
    def _():
        wbf_ref[...] = w_ref[...].astype(BF16)

    km = hm_ref.shape[1]
    o_ref[...] = x_ref[...] + (_dot(hm_ref[...], wbf_ref[:km, :]) + _dot(ha_ref[...], wbf_ref[km:, :]))


def _out_proj(hm, ha, w, layer, x2, tm=1024, tn=1024):
    T, km = hm.shape
    ka = ha.shape[1]
    N = x2.shape[1]
    return pl.pallas_call(
        _out_proj_kernel, grid=(N // tn, T // tm),
        in_specs=[pl.BlockSpec((tm, km), lambda n, m: (m, 0)),
                  pl.BlockSpec((tm, ka), lambda n, m: (m, 0)),
                  pl.BlockSpec((None, km + ka, tn), lambda n, m: (layer, 0, n)),
                  pl.BlockSpec((tm, tn), lambda n, m: (m, n))],
        out_specs=pl.BlockSpec((tm, tn), lambda n, m: (m, n)),
        out_shape=jax.ShapeDtypeStruct((T, N), F32),
        scratch_shapes=[pltpu.VMEM((km + ka, tn), BF16)],
        compiler_params=_params("parallel", "arbitrary"), name="out_proj")(hm, ha, w, x2)


def _down_proj_kernel(a_ref, w_ref, x_ref, o_ref, wbf_ref):
    @pl.when(pl.program_id(1) == 0)
    def _():
        wbf_ref[...] = w_ref[...].astype(BF16)

    o_ref[...] = x_ref[...] + _dot(a_ref[...], wbf_ref[...])


def _down_proj(a, w, layer, x2, tm=512, tn=512):
    T, K = a.shape
    N = x2.shape[1]
    return pl.pallas_call(
        _down_proj_kernel, grid=(N // tn, T // tm),
        in_specs=[pl.BlockSpec((tm, K), lambda n, m: (m, 0)),
                  pl.BlockSpec((None, K, tn), lambda n, m: (layer, 0, n)),
                  pl.BlockSpec((tm, tn), lambda n, m: (m, n))],
        out_specs=pl.BlockSpec((tm, tn), lambda n, m: (m, n)),
        out_shape=jax.ShapeDtypeStruct((T, N), F32),
        scratch_shapes=[pltpu.VMEM((K, tn), BF16)],
        compiler_params=_params("parallel", "arbitrary"), name="down_proj")(a, w, x2)


def _up_proj_kernel(h_ref, wg_ref, wv_ref, cwg_ref, cwv_ref, cbg_ref, cbv_ref, a_ref,
                    wgbf_ref, wvbf_ref, carry_g_ref, carry_v_ref, *, tiles_per_seq):
    m = pl.program_id(1)

    @pl.when(m == 0)
    def _():
        wgbf_ref[...] = wg_ref[...].astype(BF16)
        wvbf_ref[...] = wv_ref[...].astype(BF16)

    @pl.when(m % tiles_per_seq == 0)
    def _():
        carry_g_ref[...] = jnp.zeros_like(carry_g_ref)
        carry_v_ref[...] = jnp.zeros_like(carry_v_ref)

    h = h_ref[...]
    tm = h.shape[0]
    row = lax.broadcasted_iota(jnp.int32, (tm, 1), 0)

    def conv(u, cw_ref, cb_ref, carry_ref):
        tail = carry_ref[...]
        p1 = jnp.where(row == 0, tail[HALO - 1:HALO, :], pltpu.roll(u, 1, 0))
        p2 = jnp.where(row == 0, tail[HALO - 2:HALO - 1, :],
                       jnp.where(row == 1, tail[HALO - 1:HALO, :], pltpu.roll(u, 2, 0)))
        carry_ref[...] = u[tm - HALO:, :]
        cw = cw_ref[...]
        return ((cb_ref[...] + cw[0:1, :] * p2) + cw[1:2, :] * p1) + cw[2:3, :] * u

    yg = conv(_dot(h, wgbf_ref[...]), cwg_ref, cbg_ref, carry_g_ref)
    yv = conv(_dot(h, wvbf_ref[...]), cwv_ref, cbv_ref, carry_v_ref)
    a_ref[...] = (yg * _sigmoid(yg) * yv).astype(BF16)


def _up_proj(h, w, layer, cw, cb, seq_len, tm=512, tf=512):
    T, K = h.shape
    d_ff = w.shape[2] // 2
    nf = d_ff // tf
    assert d_ff % tf == 0 and seq_len % tm == 0
    kern = functools.partial(_up_proj_kernel, tiles_per_seq=seq_len // tm)
    return pl.pallas_call(
        kern, grid=(nf, T // tm),
        in_specs=[pl.BlockSpec((tm, K), lambda n, m: (m, 0)),
                  pl.BlockSpec((None, K, tf), lambda n, m: (layer, 0, n)),
                  pl.BlockSpec((None, K, tf), lambda n, m: (layer, 0, nf + n)),
                  pl.BlockSpec((FFN_CONV, tf), lambda n, m: (0, n)),
                  pl.BlockSpec((FFN_CONV, tf), lambda n, m: (0, nf + n)),
                  pl.BlockSpec((1, tf), lambda n, m: (0, n)),
                  pl.BlockSpec((1, tf), lambda n, m: (0, nf + n))],
        out_specs=pl.BlockSpec((tm, tf), lambda n, m: (m, n)),
        out_shape=jax.ShapeDtypeStruct((T, d_ff), BF16),
        scratch_shapes=[pltpu.VMEM((K, tf), BF16), pltpu.VMEM((K, tf), BF16),
                        pltpu.VMEM((HALO, tf), F32), pltpu.VMEM((HALO, tf), F32)],
        compiler_params=_params("parallel", "arbitrary"), name="up_proj")(
            h, w, w, cw, cw, cb.reshape(1, -1), cb.reshape(1, -1))


def _mlstm_kernel(q_ref, k_ref, v_ref, o_ref, g_ref, gb_ref, cwq_ref, cwk_ref, cbq_ref, cbk_ref, nw_ref,
                  out_ref, pad_ref, qs_ref, ks_ref, sv_ref, kv_ref, *, chunk):
    head = pl.program_id(1)
    S, dh = q_ref.shape[1], q_ref.shape[2]
    L = chunk
    nc = S // L
    halo = 8

    def conv_silu(x_ref, cw_ref, cb_ref, dst_ref, scale):
        pad_ref[0:halo, :] = jnp.zeros((halo, dh), F32)
        pad_ref[halo:halo + S, :] = x_ref[0]
        cw = cw_ref[...]
        cb = cb_ref[...]
        for c in range(nc):
            y = cb
            for j in range(MLSTM_CONV):
                off = halo + c * L - (MLSTM_CONV - 1) + j
                y = y + cw[j:j + 1, :] * pad_ref[off:off + L, :]
            dst_ref[c * L:(c + 1) * L, :] = (y * _sigmoid(y) * scale).astype(BF16)

    conv_silu(q_ref, cwq_ref, cbq_ref, qs_ref, dh ** -0.5)
    conv_silu(k_ref, cwk_ref, cbk_ref, ks_ref, 1.0)

    rr = lax.broadcasted_iota(jnp.int32, (L, L), 0)
    cc = lax.broadcasted_iota(jnp.int32, (L, L), 1)
    tri = rr >= cc
    eye = rr == cc
    tri_bf = jnp.where(tri, 1.0, 0.0).astype(BF16)
    lane = lax.broadcasted_iota(jnp.int32, (L, LANES), 1)

    def lane_pick(x, idx):
        return jnp.sum(jnp.where(lane == idx, x, 0.0), axis=1, keepdims=True)

    def local(c):
        sl = slice(c * L, (c + 1) * L)
        G = g_ref[0, sl, :] + gb_ref[...]
        li = lane_pick(G, head)
        LF = jnp.minimum(G, 0.0) - jnp.log1p(jnp.exp(-jnp.abs(G)))
        hi, mid, lo = _split3(LF)
        b = lane_pick(_dot(tri_bf, hi) + _dot(tri_bf, mid) + _dot(tri_bf, lo), MLSTM_HEADS + head)
        r_row = jnp.sum(jnp.where(eye, li - b, 0.0), axis=0, keepdims=True)
        Dm = jnp.where(tri, b + r_row, -jnp.inf)
        mrow = jnp.max(Dm, axis=1, keepdims=True)
        qc, kc, vc = qs_ref[sl, :], ks_ref[sl, :], v_ref[0, sl, :]
        s0 = _dot_nt(qc, kc) * jnp.exp(Dm - mrow)
        sv_ref[c] = _dot(s0.astype(BF16), vc.astype(BF16))
        bL = b[L - 1:L, :]
        gcol = bL - b + li
        gmax = jnp.max(gcol, axis=0, keepdims=True)
        wk0 = jnp.exp(gcol - gmax)
        kv_ref[c] = _dot_tn(kc, (wk0 * vc).astype(BF16))
        ksum = jnp.sum(wk0 * kc.astype(F32), axis=0, keepdims=True)
        return b, mrow, jnp.sum(s0, axis=1, keepdims=True), bL, gmax, ksum

    C = jnp.zeros((dh, dh), F32)
    n = jnp.zeros((1, dh), F32)
    m = jnp.zeros((1, 1), F32)
    for c in range(nc):
        sl = slice(c * L, (c + 1) * L)
        b, mrow, rs0, bL, gmax, ksum = local(c)
        inter = b + m
        m_t = jnp.maximum(inter, mrow)
        w_inter = jnp.exp(inter - m_t)
        f = jnp.exp(mrow - m_t)
        qc = qs_ref[sl, :]
        num = w_inter * _dot(qc, C.astype(BF16)) + f * sv_ref[c]
        den = w_inter * jnp.sum(qc.astype(F32) * n, axis=1, keepdims=True) + f * rs0
        hraw = num * (1.0 / jnp.maximum(jnp.abs(den), jnp.exp(-m_t)))
        y = _rms(hraw, nw_ref[...])
        out_ref[0, sl, :] = (y * _sigmoid(o_ref[0, sl, :])).astype(BF16)

        m_new = jnp.maximum(bL + m, gmax)
        a = jnp.exp(bL + m - m_new)
        fk = jnp.exp(gmax - m_new)
        C = a * C + fk * kv_ref[c]
        n = a * n + fk * ksum
        m = m_new


def _mlstm(zA, gates, gate_bias, conv_w, conv_b, norm_w, chunk=MLSTM_CHUNK):
    B, S, four_dm = zA.shape
    dm = four_dm // 4
    H = MLSTM_HEADS
    dh = dm // H
    kern = functools.partial(_mlstm_kernel, chunk=chunk)

    def col(base):
        return pl.BlockSpec((1, S, dh), lambda b, h: (b, 0, base + h))

    return pl.pallas_call(
        kern, grid=(B, H),
        in_specs=[col(0), col(H), col(2 * H), col(3 * H),
                  pl.BlockSpec((1, S, LANES), lambda b, h: (b, 0, 0)),
                  pl.BlockSpec((1, LANES), lambda b, h: (0, 0)),
                  pl.BlockSpec((MLSTM_CONV, dh), lambda b, h: (0, h)),
                  pl.BlockSpec((MLSTM_CONV, dh), lambda b, h: (0, H + h)),
                  pl.BlockSpec((1, dh), lambda b, h: (0, h)),
                  pl.BlockSpec((1, dh), lambda b, h: (0, H + h)),
                  pl.BlockSpec((None, 1, dh), lambda b, h: (h, 0, 0))],
        out_specs=pl.BlockSpec((1, S, dh), lambda b, h: (b, 0, h)),
        out_shape=jax.ShapeDtypeStruct((B, S, dm), BF16),
        scratch_shapes=[pltpu.VMEM((S + 8, dh), F32), pltpu.VMEM((S, dh), BF16), pltpu.VMEM((S, dh), BF16),
                        pltpu.VMEM((S // chunk, chunk, dh), F32), pltpu.VMEM((S // chunk, dh, dh), F32)],
        compiler_params=_params("parallel", "parallel"), name="mlstm")(
            zA, zA, zA, zA, gates, gate_bias, conv_w, conv_w, conv_b.reshape(1, -1), conv_b.reshape(1, -1),
            norm_w.reshape(H, 1, dh))


def _moba_kernel(rb_ref, q_ref, k_ref, v_ref, qkn_ref, out_ref,
                 bias_ref, sc_ref, qa_ref, ka_ref, va_ref):
    head = pl.program_id(0)
    S, dh = q_ref.shape[1], q_ref.shape[2]
    BLK = MOBA_BLOCK
    NB = S // BLK
    scale = dh ** -0.5

    @pl.when(pl.program_id(1) == 0)
    def _build_bias():
        rr = lax.broadcasted_iota(jnp.int32, (BLK, BLK), 0)
        cc = lax.broadcasted_iota(jnp.int32, (BLK, BLK), 1)
        for delta in range(NB):
            d = (rr - cc) + delta * BLK
            lo, hi = max(delta * BLK - (BLK - 1), 0), delta * BLK + BLK - 1
            tile = jnp.full((BLK, BLK), rb_ref[int(_T5_BUCKET[lo]), head], F32)
            for kb in range(int(_T5_BUCKET[lo]) + 1, int(_T5_BUCKET[hi]) + 1):
                tile = jnp.where(d >= _T5_THR[kb], rb_ref[kb, head], tile)
            if delta == 0:
                tile = jnp.where(d >= 0, tile, -jnp.inf)
            bias_ref[delta] = tile

    qn = _rms(q_ref[0], qkn_ref[0:1, :])
    kn = _rms(k_ref[0], qkn_ref[1:2, :])
    q_hi = qn.astype(BF16)
    qa_ref[:, :, :dh] = q_hi.reshape(NB, BLK, dh)
    lane_k = lax.broadcasted_iota(jnp.int32, (BLK, dh), 1)
    ka_ref[:, :, :dh] = kn.astype(BF16).reshape(NB, BLK, dh)
    for j in range(NB):
        ka_ref[j, :, dh:] = jnp.where(lane_k == j, 1.0, 0.0).astype(BF16)
    va_ref[:, :, :dh] = v_ref[0].astype(BF16).reshape(NB, BLK, dh)
    va_ref[:, :, dh:] = jnp.ones((NB, BLK, dh), BF16)

    kmean = jnp.mean(kn.reshape(NB, BLK, dh), axis=1)
    kmean = jnp.concatenate([kmean, jnp.zeros((16 - NB, dh), F32)], axis=0)
    q_lo = (qn - q_hi.astype(F32)).astype(BF16)
    m_hi = kmean.astype(BF16)
    m_lo = (kmean - m_hi.astype(F32)).astype(BF16)
    gate_t = _dot_nt(m_hi, q_hi) + (_dot_nt(m_lo, q_hi) + _dot_nt(m_hi, q_lo))

    blk = lax.broadcasted_iota(jnp.int32, (NB, BLK), 0)
    for i in range(1, NB):
        g = gate_t[:NB, i * BLK:(i + 1) * BLK]
        cnt = jnp.zeros((NB, BLK), F32)
        for jp in range(i):
            gj = g[jp:jp + 1, :]
            beats = (gj > g) | ((gj == g) & (blk > jp))
            cnt = cnt + jnp.where(beats, 1.0, 0.0)
        mask_t = jnp.where((cnt < MOBA_TOPK) | (blk >= i), 0.0, MASKED)
        mask_t = jnp.concatenate([mask_t, jnp.zeros((dh - NB, BLK), F32)], axis=0)
        qa_ref[i, :, dh:] = mask_t.T.astype(BF16)
    qa_ref[0, :, dh:] = jnp.zeros((BLK, dh), BF16)

    tile = lambda i, j: i * (i + 1) // 2 + j
    mxs = []
    for i in range(NB):
        qa = qa_ref[i]
        mx = None
        for j in range(i + 1):
            s = _dot_nt(qa, ka_ref[j]) * scale + bias_ref[i - j]
            sc_ref[tile(i, j)] = s
            mj = jnp.maximum(s[:, :LANES], s[:, LANES:])
            mx = mj if mx is None else jnp.maximum(mx, mj)
        mxs.append(jnp.max(mx, axis=1, keepdims=True))

    for i in range(NB):
        acc = jnp.zeros((BLK, 2 * dh), F32)
        for j in range(i + 1):
            acc = acc + _dot(jnp.exp(sc_ref[tile(i, j)] - mxs[i]).astype(BF16), va_ref[j])
        out_ref[0, i * BLK:(i + 1) * BLK, :] = (acc[:, :dh] * (1.0 / acc[:, dh:dh + 1])).astype(BF16)


def _moba(zB, qk_norm_l, rel_bias):
    B, S, three_da = zB.shape
    da = three_da // 3
    H = MOBA_HEADS
    dh = da // H
    NB = S // MOBA_BLOCK
    BLK = MOBA_BLOCK

    def col(base):
        return pl.BlockSpec((1, S, dh), lambda h, b: (b, 0, base + h))

    return pl.pallas_call(
        _moba_kernel, grid=(H, B),
        in_specs=[pl.BlockSpec(memory_space=pltpu.SMEM),
                  col(0), col(H), col(2 * H),
                  pl.BlockSpec((2, dh), lambda h, b: (0, 0))],
        out_specs=pl.BlockSpec((1, S, dh), lambda h, b: (b, 0, h)),
        out_shape=jax.ShapeDtypeStruct((B, S, da), BF16),
        scratch_shapes=[pltpu.VMEM((NB, BLK, BLK), F32), pltpu.VMEM((NB * (NB + 1) // 2, BLK, BLK), F32),
                        pltpu.VMEM((NB, BLK, 2 * dh), BF16), pltpu.VMEM((NB, BLK, 2 * dh), BF16),
                        pltpu.VMEM((NB, BLK, 2 * dh), BF16)],
        compiler_params=_params("parallel", "arbitrary"), name="moba")(rel_bias, zB, zB, zB, qk_norm_l)


def kernel(x, norm_mix, w_in, gate_bias, conv_qk_w, conv_qk_b, mlstm_norm, qk_norm, rel_bias, w_out, norm_ffn,
           w_up, conv_ffn_w, conv_ffn_b, w_down):
    B, S, D = x.shape
    depth = w_in.shape[0]
    H = MLSTM_HEADS
    dm = mlstm_norm.shape[1] * mlstm_norm.shape[2]
    da = w_out.shape[1] - dm
    n_gate = 2 * H
    g0 = 4 * dm
    assert w_in.shape[2] == g0 + n_gate + 3 * da

    gb = jnp.pad(gate_bias, ((0, 0), (0, LANES - n_gate)))

    x2 = x.reshape(B * S, D)
    for l in range(depth):
        h, gates = _norm(x2, norm_mix[l], w_in, l, g0)
        zA = _proj(h, w_in, l, 0, g0)
        zB = _proj(h, w_in, l, g0 + n_gate, 3 * da)
        hm = _mlstm(zA.reshape(B, S, g0), gates.reshape(B, S, LANES), gb[l:l + 1], conv_qk_w[l], conv_qk_b[l],
                    mlstm_norm[l])
        ha = _moba(zB.reshape(B, S, 3 * da), qk_norm[l], rel_bias)
        x2 = _out_proj(hm.reshape(B * S, dm), ha.reshape(B * S, da), w_out, l, x2)
        h2 = _norm(x2, norm_ffn[l])
        a = _up_proj(h2, w_up, l, conv_ffn_w[l], conv_ffn_b[l], S)
        x2 = _down_proj(a, w_down, l, x2)
    return x2.reshape(B, S, D)
```

```python
import functools
import math

import numpy as np
import jax
import jax.numpy as jnp
from jax import lax
from jax.experimental import pallas as pl
from jax.experimental.pallas import tpu as pltpu

F32 = jnp.float32
BF16 = jnp.bfloat16

EPS = 1e-6
MLSTM_HEADS = 4
MLSTM_CONV = 4
MLSTM_CHUNK = 256
MOBA_HEADS = 8
MOBA_BLOCK = 256
MOBA_TOPK = 3
MASKED = -1e30
N_BUCKETS = 32
MAX_DISTANCE = 2048
FFN_CONV = 3
HALO = 8

LANES = 128
VMEM_LIMIT = 56 * 1024 * 1024


def _t5_thresholds():
    max_exact = N_BUCKETS // 2
    d = np.arange(0, MAX_DISTANCE, dtype=np.float64)
    ratio = np.maximum(d, max_exact) / max_exact
    large = max_exact + (np.log(ratio) / math.log(MAX_DISTANCE / max_exact)
                         * (N_BUCKETS - max_exact)).astype(np.int64)
    bucket = np.where(d < max_exact, d.astype(np.int64), np.minimum(large, N_BUCKETS - 1))
    assert np.all(np.diff(bucket) >= 0)
    thr = [int(np.argmax(bucket >= k)) if np.any(bucket >= k) else MAX_DISTANCE for k in range(N_BUCKETS)]
    return bucket, thr


_T5_BUCKET, _T5_THR = _t5_thresholds()


def _sigmoid(x):
    return 1.0 / (1.0 + jnp.exp(-x))


def _split3(x):
    hi = x.astype(BF16)
    r1 = x - hi.astype(F32)
    mid = r1.astype(BF16)
    lo = (r1 - mid.astype(F32)).astype(BF16)
    return hi, mid, lo


def _dot(a, b):
    return jnp.dot(a, b, preferred_element_type=F32)


def _dot_nt(a, b):
    return lax.dot_general(a, b, (((1,), (1,)), ((), ())), preferred_element_type=F32)


def _dot_tn(a, b):
    return lax.dot_general(a, b, (((0,), (0,)), ((), ())), preferred_element_type=F32)


def _params(*sem, flags=None):
    return pltpu.CompilerParams(dimension_semantics=sem, vmem_limit_bytes=VMEM_LIMIT, flags=flags)


def _rms(x, g):
    return x * lax.rsqrt(jnp.mean(x * x, axis=-1, keepdims=True) + EPS) * g


def _norm_kernel(x_ref, g_ref, h_ref):
    h_ref[...] = _rms(x_ref[...], g_ref[...]).astype(BF16)


def _norm_gates_kernel(x_ref, g_ref, wg_ref, h_ref, gates_ref):
    h = _rms(x_ref[...], g_ref[...]).astype(BF16)
    h_ref[...] = h
    gates_ref[...] = _dot_nt(h, wg_ref[...].astype(BF16))


def _norm(x2, g, wt=None, layer=0, gate_row=None, tm=512):
    T, D = x2.shape
    x_spec = pl.BlockSpec((tm, D), lambda i: (i, 0))
    g_spec = pl.BlockSpec((1, D), lambda i: (0, 0))
    h_spec = pl.BlockSpec((tm, D), lambda i: (i, 0))
    if wt is None:
        return pl.pallas_call(
            _norm_kernel, grid=(T // tm,), in_specs=[x_spec, g_spec], out_specs=h_spec,
            out_shape=jax.ShapeDtypeStruct((T, D), BF16), compiler_params=_params("parallel"),
            name="rmsnorm")(x2, g.reshape(1, D))
    assert gate_row % LANES == 0
    return pl.pallas_call(
        _norm_gates_kernel, grid=(T // tm,),
        in_specs=[x_spec, g_spec, pl.BlockSpec((None, LANES, D), lambda i: (layer, gate_row // LANES, 0))],
        out_specs=[h_spec, pl.BlockSpec((tm, LANES), lambda i: (i, 0))],
        out_shape=[jax.ShapeDtypeStruct((T, D), BF16), jax.ShapeDtypeStruct((T, LANES), F32)],
        compiler_params=_params("parallel"), name="rmsnorm_gates")(x2, g.reshape(1, D), wt)


def _proj_t_kernel(h_ref, w_ref, o_ref, wbf_ref):
    @pl.when(pl.program_id(1) == 0)
    def _():
        wbf_ref[...] = w_ref[...].astype(BF16)

    o_ref[...] = _dot_nt(h_ref[...], wbf_ref[...]).astype(o_ref.dtype)


def _proj_t_shifted_kernel(h_ref, w_ref, wnext_ref, o_ref, wbf_ref, *, shift):
    @pl.when(pl.program_id(1) == 0)
    def _():
        wbf_ref[...] = jnp.concatenate([w_ref[shift:, :], wnext_ref[...]], axis=0).astype(BF16)

    o_ref[...] = _dot_nt(h_ref[...], wbf_ref[...]).astype(o_ref.dtype)


def _proj_t(h, wt, layer, row0, nrows, tm=1024, tn=1024, out_dtype=F32):
    T, K = h.shape
    shift = row0 % tn
    base = row0 - shift
    assert nrows % tn == 0 and T % tm == 0
    r0 = base // tn
    h_spec = pl.BlockSpec((tm, K), lambda n, m: (m, 0))
    w_spec = pl.BlockSpec((None, tn, K), lambda n, m: (layer, r0 + n, 0))
    common = dict(
        grid=(nrows // tn, T // tm),
        out_specs=pl.BlockSpec((tm, tn), lambda n, m: (m, n)),
        out_shape=jax.ShapeDtypeStruct((T, nrows), out_dtype),
        scratch_shapes=[pltpu.VMEM((tn, K), BF16)],
        compiler_params=_params("parallel", "arbitrary"), name="in_proj")
    if shift == 0:
        return pl.pallas_call(_proj_t_kernel, in_specs=[h_spec, w_spec], **common)(h, wt)
    assert shift % 16 == 0 or shift == 8, shift
    assert tn % shift == 0 and base % shift == 0
    per_tile = tn // shift
    wnext_spec = pl.BlockSpec((None, shift, K), lambda n, m: (layer, (r0 + n + 1) * per_tile, 0))
    return pl.pallas_call(functools.partial(_proj_t_shifted_kernel, shift=shift),
                          in_specs=[h_spec, w_spec, wnext_spec], **common)(h, wt, wt)


def _out_proj_kernel(hm_ref, ha_ref, w_ref, x_ref, o_ref, wbf_ref):
    @pl.when(pl.program_id(1) == 0)
    def _():
        wbf_ref[...] = w_ref[...].astype(BF16)

    km = hm_ref.shape[1]
    o_ref[...] = x_ref[...] + (_dot(hm_ref[...], wbf_ref[:km, :]) + _dot(ha_ref[...], wbf_ref[km:, :]))


def _out_proj(hm, ha, w, layer, x2, tm=1024, tn=1024):
    T, km = hm.shape
    ka = ha.shape[1]
    N = x2.shape[1]
    return pl.pallas_call(
        _out_proj_kernel, grid=(N // tn, T // tm),
        in_specs=[pl.BlockSpec((tm, km), lambda n, m: (m, 0)),
                  pl.BlockSpec((tm, ka), lambda n, m: (m, 0)),
                  pl.BlockSpec((None, km + ka, tn), lambda n, m: (layer, 0, n)),
                  pl.BlockSpec((tm, tn), lambda n, m: (m, n))],
        out_specs=pl.BlockSpec((tm, tn), lambda n, m: (m, n)),
        out_shape=jax.ShapeDtypeStruct((T, N), F32),
        scratch_shapes=[pltpu.VMEM((km + ka, tn), BF16)],
        compiler_params=_params("parallel", "arbitrary"), name="out_proj")(hm, ha, w, x2)


def _down_proj_kernel(a_ref, w_ref, x_ref, o_ref, wbf_ref):
    @pl.when(pl.program_id(1) == 0)
    def _():
        wbf_ref[...] = w_ref[...].astype(BF16)

    o_ref[...] = x_ref[...] + _dot(a_ref[...], wbf_ref[...])


def _down_proj(a, w, layer, x2, tm=512, tn=512):
    T, K = a.shape
    N = x2.shape[1]
    return pl.pallas_call(
        _down_proj_kernel, grid=(N // tn, T // tm),
        in_specs=[pl.BlockSpec((tm, K), lambda n, m: (m, 0)),
                  pl.BlockSpec((None, K, tn), lambda n, m: (layer, 0, n)),
                  pl.BlockSpec((tm, tn), lambda n, m: (m, n))],
        out_specs=pl.BlockSpec((tm, tn), lambda n, m: (m, n)),
        out_shape=jax.ShapeDtypeStruct((T, N), F32),
        scratch_shapes=[pltpu.VMEM((K, tn), BF16)],
        compiler_params=_params("parallel", "arbitrary"), name="down_proj")(a, w, x2)


def _up_proj_kernel(h_ref, wg_ref, wv_ref, cwg_ref, cwv_ref, cbg_ref, cbv_ref, a_ref,
                    wgbf_ref, wvbf_ref, carry_g_ref, carry_v_ref, *, tiles_per_seq):
    m = pl.program_id(1)

    @pl.when(m == 0)
    def _():
        wgbf_ref[...] = wg_ref[...].astype(BF16)
        wvbf_ref[...] = wv_ref[...].astype(BF16)

    @pl.when(m % tiles_per_seq == 0)
    def _():
        carry_g_ref[...] = jnp.zeros_like(carry_g_ref)
        carry_v_ref[...] = jnp.zeros_like(carry_v_ref)

    h = h_ref[...]
    tm = h.shape[0]
    row = lax.broadcasted_iota(jnp.int32, (tm, 1), 0)

    def shifted(u, tail, k):
        r = pltpu.roll(u, k, 0)
        top = jnp.where(row[:HALO] < k, pltpu.roll(tail, k, 0), r[:HALO])
        return jnp.concatenate([top, r[HALO:]], axis=0)

    def conv(u, cw_ref, cb_ref, carry_ref):
        tail = carry_ref[...]
        carry_ref[...] = u[tm - HALO:, :]
        cw = cw_ref[...]
        return ((cb_ref[...] + cw[0:1, :] * shifted(u, tail, 2)) + cw[1:2, :] * shifted(u, tail, 1)) \
            + cw[2:3, :] * u

    yg = conv(_dot(h, wgbf_ref[...]), cwg_ref, cbg_ref, carry_g_ref)
    yv = conv(_dot(h, wvbf_ref[...]), cwv_ref, cbv_ref, carry_v_ref)
    a_ref[...] = (yg * _sigmoid(yg) * yv).astype(BF16)


def _up_proj(h, w, layer, cw, cb, seq_len, tm=512, tf=512):
    T, K = h.shape
    d_ff = w.shape[2] // 2
    nf = d_ff // tf
    assert d_ff % tf == 0 and seq_len % tm == 0
    kern = functools.partial(_up_proj_kernel, tiles_per_seq=seq_len // tm)
    return pl.pallas_call(
        kern, grid=(nf, T // tm),
        in_specs=[pl.BlockSpec((tm, K), lambda n, m: (m, 0)),
                  pl.BlockSpec((None, K, tf), lambda n, m: (layer, 0, n)),
                  pl.BlockSpec((None, K, tf), lambda n, m: (layer, 0, nf + n)),
                  pl.BlockSpec((FFN_CONV, tf), lambda n, m: (0, n)),
                  pl.BlockSpec((FFN_CONV, tf), lambda n, m: (0, nf + n)),
                  pl.BlockSpec((1, tf), lambda n, m: (0, n)),
                  pl.BlockSpec((1, tf), lambda n, m: (0, nf + n))],
        out_specs=pl.BlockSpec((tm, tf), lambda n, m: (m, n)),
        out_shape=jax.ShapeDtypeStruct((T, d_ff), BF16),
        scratch_shapes=[pltpu.VMEM((K, tf), BF16), pltpu.VMEM((K, tf), BF16),
                        pltpu.VMEM((HALO, tf), F32), pltpu.VMEM((HALO, tf), F32)],
        compiler_params=_params("parallel", "arbitrary"), name="up_proj")(
            h, w, w, cw, cw, cb.reshape(1, -1), cb.reshape(1, -1))


def _mlstm_kernel(q_ref, k_ref, v_ref, o_ref, g_ref, gb_ref, cwq_ref, cwk_ref, cbq_ref, cbk_ref, nw_ref,
                  out_ref, pad_ref, qs_ref, ks_ref, sv_ref, kv_ref, *, chunk):
    head = pl.program_id(1)
    S, dh = q_ref.shape[1], q_ref.shape[2]
    L = chunk
    nc = S // L
    halo = 8

    def conv_silu(x_ref, cw_ref, cb_ref, dst_ref, scale):
        pad_ref[0:halo, :] = jnp.zeros((halo, dh), F32)
        pad_ref[halo:halo + S, :] = x_ref[0]
        cw = cw_ref[...]
        cb = cb_ref[...]
        for c in range(nc):
            y = cb
            for j in range(MLSTM_CONV):
                off = halo + c * L - (MLSTM_CONV - 1) + j
                y = y + cw[j:j + 1, :] * pad_ref[off:off + L, :]
            dst_ref[c * L:(c + 1) * L, :] = (y * _sigmoid(y) * scale).astype(BF16)

    conv_silu(q_ref, cwq_ref, cbq_ref, qs_ref, dh ** -0.5)
    conv_silu(k_ref, cwk_ref, cbk_ref, ks_ref, 1.0)

    rr = lax.broadcasted_iota(jnp.int32, (L, L), 0)
    cc = lax.broadcasted_iota(jnp.int32, (L, L), 1)
    tri = rr >= cc
    eye = rr == cc
    tri_bf = jnp.where(tri, 1.0, 0.0).astype(BF16)
    lane = lax.broadcasted_iota(jnp.int32, (L, LANES), 1)

    def lane_pick(x, idx):
        return jnp.sum(jnp.where(lane == idx, x, 0.0), axis=1, keepdims=True)

    def local(c):
        sl = slice(c * L, (c + 1) * L)
        G = g_ref[0, sl, :] + gb_ref[...]
        li = lane_pick(G, head)
        LF = jnp.minimum(G, 0.0) - jnp.log1p(jnp.exp(-jnp.abs(G)))
        hi, mid, lo = _split3(LF)
        b = lane_pick(_dot(tri_bf, hi) + _dot(tri_bf, mid) + _dot(tri_bf, lo), MLSTM_HEADS + head)
        r_row = jnp.sum(jnp.where(eye, li - b, 0.0), axis=0, keepdims=True)
        Dm = jnp.where(tri, b + r_row, -jnp.inf)
        mrow = jnp.max(Dm, axis=1, keepdims=True)
        qc, kc, vc = qs_ref[sl, :], ks_ref[sl, :], v_ref[0, sl, :]
        s0 = _dot_nt(qc, kc) * jnp.exp(Dm - mrow)
        sv_ref[c] = _dot(s0.astype(BF16), vc.astype(BF16))
        bL = b[L - 1:L, :]
        gcol = bL - b + li
        gmax = jnp.max(gcol, axis=0, keepdims=True)
        wk0 = jnp.exp(gcol - gmax)
        kv_ref[c] = _dot_tn(kc, (wk0 * vc).astype(BF16))
        ksum = jnp.sum(wk0 * kc.astype(F32), axis=0, keepdims=True)
        return b, mrow, jnp.sum(s0, axis=1, keepdims=True), bL, gmax, ksum

    C = jnp.zeros((dh, dh), F32)
    n = jnp.zeros((1, dh), F32)
    m = jnp.zeros((1, 1), F32)
    for c in range(nc):
        sl = slice(c * L, (c + 1) * L)
        b, mrow, rs0, bL, gmax, ksum = local(c)
        inter = b + m
        m_t = jnp.maximum(inter, mrow)
        w_inter = jnp.exp(inter - m_t)
        f = jnp.exp(mrow - m_t)
        qc = qs_ref[sl, :]
        num = w_inter * _dot(qc, C.astype(BF16)) + f * sv_ref[c]
        den = w_inter * jnp.sum(qc.astype(F32) * n, axis=1, keepdims=True) + f * rs0
        hraw = num * (1.0 / jnp.maximum(jnp.abs(den), jnp.exp(-m_t)))
        y = _rms(hraw, nw_ref[...])
        out_ref[0, sl, :] = (y * _sigmoid(o_ref[0, sl, :])).astype(BF16)

        m_new = jnp.maximum(bL + m, gmax)
        a = jnp.exp(bL + m - m_new)
        fk = jnp.exp(gmax - m_new)
        C = a * C + fk * kv_ref[c]
        n = a * n + fk * ksum
        m = m_new


def _mlstm(zA, gates, gate_bias, conv_w, conv_b, norm_w, chunk=MLSTM_CHUNK):
    B, S, four_dm = zA.shape
    dm = four_dm // 4
    H = MLSTM_HEADS
    dh = dm // H
    kern = functools.partial(_mlstm_kernel, chunk=chunk)

    def col(base):
        return pl.BlockSpec((1, S, dh), lambda b, h: (b, 0, base + h))

    return pl.pallas_call(
        kern, grid=(B, H),
        in_specs=[col(0), col(H), col(2 * H), col(3 * H),
                  pl.BlockSpec((1, S, LANES), lambda b, h: (b, 0, 0)),
                  pl.BlockSpec((1, LANES), lambda b, h: (0, 0)),
                  pl.BlockSpec((MLSTM_CONV, dh), lambda b, h: (0, h)),
                  pl.BlockSpec((MLSTM_CONV, dh), lambda b, h: (0, H + h)),
                  pl.BlockSpec((1, dh), lambda b, h: (0, h)),
                  pl.BlockSpec((1, dh), lambda b, h: (0, H + h)),
                  pl.BlockSpec((None, 1, dh), lambda b, h: (h, 0, 0))],
        out_specs=pl.BlockSpec((1, S, dh), lambda b, h: (b, 0, h)),
        out_shape=jax.ShapeDtypeStruct((B, S, dm), BF16),
        scratch_shapes=[pltpu.VMEM((S + 8, dh), F32), pltpu.VMEM((S, dh), BF16), pltpu.VMEM((S, dh), BF16),
                        pltpu.VMEM((S // chunk, chunk, dh), F32), pltpu.VMEM((S // chunk, dh, dh), F32)],
        compiler_params=_params("parallel", "parallel"), name="mlstm")(
            zA, zA, zA, zA, gates, gate_bias, conv_w, conv_w, conv_b.reshape(1, -1), conv_b.reshape(1, -1),
            norm_w.reshape(H, 1, dh))


def _moba_kernel(rb_ref, q_ref, k_ref, v_ref, qkn_ref, out_ref,
                 bias_ref, sc_ref, qa_ref, ka_ref, va_ref):
    head = pl.program_id(0)
    S, dh = q_ref.shape[1], q_ref.shape[2]
    BLK = MOBA_BLOCK
    NB = S // BLK
    scale = dh ** -0.5

    @pl.when(pl.program_id(1) == 0)
    def _build_bias():
        rr = lax.broadcasted_iota(jnp.int32, (BLK, BLK), 0)
        cc = lax.broadcasted_iota(jnp.int32, (BLK, BLK), 1)
        for delta in range(NB):
            d = (rr - cc) + delta * BLK
            lo, hi = max(delta * BLK - (BLK - 1), 0), delta * BLK + BLK - 1
            tile = jnp.full((BLK, BLK), rb_ref[int(_T5_BUCKET[lo]), head], F32)
            for kb in range(int(_T5_BUCKET[lo]) + 1, int(_T5_BUCKET[hi]) + 1):
                tile = jnp.where(d >= _T5_THR[kb], rb_ref[kb, head], tile)
            if delta == 0:
                tile = jnp.where(d >= 0, tile, -jnp.inf)
            bias_ref[delta] = tile

    qn = _rms(q_ref[0], qkn_ref[0:1, :])
    kn = _rms(k_ref[0], qkn_ref[1:2, :])
    q_hi = qn.astype(BF16)
    qa_ref[:, :, :dh] = q_hi.reshape(NB, BLK, dh)
    lane_k = lax.broadcasted_iota(jnp.int32, (BLK, dh), 1)
    ka_ref[:, :, :dh] = kn.astype(BF16).reshape(NB, BLK, dh)
    for j in range(NB):
        ka_ref[j, :, dh:] = jnp.where(lane_k == j, 1.0, 0.0).astype(BF16)
    va_ref[:, :, :dh] = v_ref[0].astype(BF16).reshape(NB, BLK, dh)
    va_ref[:, :, dh:] = jnp.ones((NB, BLK, dh), BF16)

    kmean = jnp.mean(kn.reshape(NB, BLK, dh), axis=1)
    kmean = jnp.concatenate([kmean, jnp.zeros((16 - NB, dh), F32)], axis=0)
    q_lo = (qn - q_hi.astype(F32)).astype(BF16)
    m_hi = kmean.astype(BF16)
    m_lo = (kmean - m_hi.astype(F32)).astype(BF16)
    gate_t = _dot_nt(m_hi, q_hi) + (_dot_nt(m_lo, q_hi) + _dot_nt(m_hi, q_lo))

    blk = lax.broadcasted_iota(jnp.int32, (NB, BLK), 0)
    for i in range(1, NB):
        g = gate_t[:NB, i * BLK:(i + 1) * BLK]
        cnt = jnp.zeros((NB, BLK), F32)
        for jp in range(i):
            gj = g[jp:jp + 1, :]
            beats = (gj > g) | ((gj == g) & (blk > jp))
            cnt = cnt + jnp.where(beats, 1.0, 0.0)
        mask_t = jnp.where((cnt < MOBA_TOPK) | (blk >= i), 0.0, MASKED)
        mask_t = jnp.concatenate([mask_t, jnp.zeros((dh - NB, BLK), F32)], axis=0)
        qa_ref[i, :, dh:] = mask_t.T.astype(BF16)
    qa_ref[0, :, dh:] = jnp.zeros((BLK, dh), BF16)

    tile = lambda i, j: i * (i + 1) // 2 + j
    mxs = []
    for i in range(NB):
        qa = qa_ref[i]
        mx = None
        for j in range(i + 1):
            s = _dot_nt(qa, ka_ref[j]) * scale + bias_ref[i - j]
            sc_ref[tile(i, j)] = s
            mj = jnp.maximum(s[:, :LANES], s[:, LANES:])
            mx = mj if mx is None else jnp.maximum(mx, mj)
        mxs.append(jnp.max(mx, axis=1, keepdims=True))

    for i in range(NB):
        acc = jnp.zeros((BLK, 2 * dh), F32)
        for j in range(i + 1):
            acc = acc + _dot(jnp.exp(sc_ref[tile(i, j)] - mxs[i]).astype(BF16), va_ref[j])
        out_ref[0, i * BLK:(i + 1) * BLK, :] = (acc[:, :dh] * (1.0 / acc[:, dh:dh + 1])).astype(BF16)


def _moba(zB, qk_norm_l, rel_bias):
    B, S, three_da = zB.shape
    da = three_da // 3
    H = MOBA_HEADS
    dh = da // H
    NB = S // MOBA_BLOCK
    BLK = MOBA_BLOCK

    def col(base):
        return pl.BlockSpec((1, S, dh), lambda h, b: (b, 0, base + h))

    return pl.pallas_call(
        _moba_kernel, grid=(H, B),
        in_specs=[pl.BlockSpec(memory_space=pltpu.SMEM),
                  col(0), col(H), col(2 * H),
                  pl.BlockSpec((2, dh), lambda h, b: (0, 0))],
        out_specs=pl.BlockSpec((1, S, dh), lambda h, b: (b, 0, h)),
        out_shape=jax.ShapeDtypeStruct((B, S, da), BF16),
        scratch_shapes=[pltpu.VMEM((NB, BLK, BLK), F32), pltpu.VMEM((NB * (NB + 1) // 2, BLK, BLK), F32),
                        pltpu.VMEM((NB, BLK, 2 * dh), BF16), pltpu.VMEM((NB, BLK, 2 * dh), BF16),
                        pltpu.VMEM((NB, BLK, 2 * dh), BF16)],
        compiler_params=_params("parallel", "arbitrary"), name="moba")(rel_bias, zB, zB, zB, qk_norm_l)


def kernel(x, norm_mix, w_in, gate_bias, conv_qk_w, conv_qk_b, mlstm_norm, qk_norm, rel_bias, w_out, norm_ffn,
           w_up, conv_ffn_w, conv_ffn_b, w_down):
    B, S, D = x.shape
    depth = w_in.shape[0]
    H = MLSTM_HEADS
    dm = mlstm_norm.shape[1] * mlstm_norm.shape[2]
    da = w_out.shape[1] - dm
    n_gate = 2 * H
    g0 = 4 * dm
    assert w_in.shape[2] == g0 + n_gate + 3 * da

    gb = jnp.pad(gate_bias, ((0, 0), (0, LANES - n_gate)))

    w_in_t = jnp.swapaxes(w_in, 1, 2)

    x2 = x.reshape(B * S, D)
    for l in range(depth):
        h, gates = _norm(x2, norm_mix[l], w_in_t, l, g0)
        zA = _proj_t(h, w_in_t, l, 0, g0)
        zB = _proj_t(h, w_in_t, l, g0 + n_gate, 3 * da)
        hm = _mlstm(zA.reshape(B, S, g0), gates.reshape(B, S, LANES), gb[l:l + 1], conv_qk_w[l], conv_qk_b[l],
                    mlstm_norm[l])
        ha = _moba(zB.reshape(B, S, 3 * da), qk_norm[l], rel_bias)
        x2 = _out_proj(hm.reshape(B * S, dm), ha.reshape(B * S, da), w_out, l, x2)
        h2 = _norm(x2, norm_ffn[l])
        a = _up_proj(h2, w_up, l, conv_ffn_w[l], conv_ffn_b[l], S)
        x2 = _down_proj(a, w_down, l, x2)
    return x2.reshape(B, S, D)
```

```python
import functools
import math

import numpy as np
import jax
import jax.numpy as jnp
from jax import lax
from jax.experimental import pallas as pl
from jax.experimental.pallas import tpu as pltpu

F32 = jnp.float32
BF16 = jnp.bfloat16

EPS = 1e-6
MLSTM_HEADS = 4
MLSTM_CONV = 4
MLSTM_CHUNK = 256
MOBA_HEADS = 8
MOBA_BLOCK = 256
MOBA_TOPK = 3
MASKED = -1e30
N_BUCKETS = 32
MAX_DISTANCE = 2048
FFN_CONV = 3
HALO = 8
UP_ROWS = 256
UP_COLS = 256

LANES = 128
VMEM_LIMIT = 56 * 1024 * 1024


def _t5_thresholds():
    max_exact = N_BUCKETS // 2
    d = np.arange(0, MAX_DISTANCE, dtype=np.float64)
    ratio = np.maximum(d, max_exact) / max_exact
    large = max_exact + (np.log(ratio) / math.log(MAX_DISTANCE / max_exact)
                         * (N_BUCKETS - max_exact)).astype(np.int64)
    bucket = np.where(d < max_exact, d.astype(np.int64), np.minimum(large, N_BUCKETS - 1))
    assert np.all(np.diff(bucket) >= 0)
    thr = [int(np.argmax(bucket >= k)) if np.any(bucket >= k) else MAX_DISTANCE for k in range(N_BUCKETS)]
    return bucket, thr


_T5_BUCKET, _T5_THR = _t5_thresholds()


def _sigmoid(x):
    return 1.0 / (1.0 + jnp.exp(-x))


def _split3(x):
    hi = x.astype(BF16)
    r1 = x - hi.astype(F32)
    mid = r1.astype(BF16)
    lo = (r1 - mid.astype(F32)).astype(BF16)
    return hi, mid, lo


def _dot(a, b):
    return jnp.dot(a, b, preferred_element_type=F32)


def _dot_nt(a, b):
    return lax.dot_general(a, b, (((1,), (1,)), ((), ())), preferred_element_type=F32)


def _dot_tn(a, b):
    return lax.dot_general(a, b, (((0,), (0,)), ((), ())), preferred_element_type=F32)


def _params(*sem, flags=None):
    return pltpu.CompilerParams(dimension_semantics=sem, vmem_limit_bytes=VMEM_LIMIT, flags=flags)


def _rms(x, g):
    return x * lax.rsqrt(jnp.mean(x * x, axis=-1, keepdims=True) + EPS) * g


def _norm_kernel(x_ref, g_ref, h_ref):
    h_ref[...] = _rms(x_ref[...], g_ref[...]).astype(BF16)


def _norm_gates_kernel(x_ref, g_ref, wg_ref, h_ref, gates_ref):
    h = _rms(x_ref[...], g_ref[...]).astype(BF16)
    h_ref[...] = h
    gates_ref[...] = _dot_nt(h, wg_ref[...].astype(BF16))


def _norm(x2, g, wt=None, layer=0, gate_row=None, tm=512):
    T, D = x2.shape
    x_spec = pl.BlockSpec((tm, D), lambda i: (i, 0))
    g_spec = pl.BlockSpec((1, D), lambda i: (0, 0))
    h_spec = pl.BlockSpec((tm, D), lambda i: (i, 0))
    if wt is None:
        return pl.pallas_call(
            _norm_kernel, grid=(T // tm,), in_specs=[x_spec, g_spec], out_specs=h_spec,
            out_shape=jax.ShapeDtypeStruct((T, D), BF16), compiler_params=_params("parallel"),
            name="rmsnorm")(x2, g.reshape(1, D))
    assert gate_row % LANES == 0
    return pl.pallas_call(
        _norm_gates_kernel, grid=(T // tm,),
        in_specs=[x_spec, g_spec, pl.BlockSpec((None, LANES, D), lambda i: (layer, gate_row // LANES, 0))],
        out_specs=[h_spec, pl.BlockSpec((tm, LANES), lambda i: (i, 0))],
        out_shape=[jax.ShapeDtypeStruct((T, D), BF16), jax.ShapeDtypeStruct((T, LANES), F32)],
        compiler_params=_params("parallel"), name="rmsnorm_gates")(x2, g.reshape(1, D), wt)


def _proj_t_kernel(h_ref, w_ref, o_ref, wbf_ref):
    @pl.when(pl.program_id(1) == 0)
    def _():
        wbf_ref[...] = w_ref[...].astype(BF16)

    o_ref[...] = _dot_nt(h_ref[...], wbf_ref[...]).astype(o_ref.dtype)


def _proj_t_shifted_kernel(h_ref, w_ref, wnext_ref, o_ref, wbf_ref, *, shift):
    @pl.when(pl.program_id(1) == 0)
    def _():
        wbf_ref[...] = jnp.concatenate([w_ref[shift:, :], wnext_ref[...]], axis=0).astype(BF16)

    o_ref[...] = _dot_nt(h_ref[...], wbf_ref[...]).astype(o_ref.dtype)


def _proj_t(h, wt, layer, row0, nrows, tm=1024, tn=1024, out_dtype=F32):
    T, K = h.shape
    shift = row0 % tn
    base = row0 - shift
    assert nrows % tn == 0 and T % tm == 0
    r0 = base // tn
    h_spec = pl.BlockSpec((tm, K), lambda n, m: (m, 0))
    w_spec = pl.BlockSpec((None, tn, K), lambda n, m: (layer, r0 + n, 0))
    common = dict(
        grid=(nrows // tn, T // tm),
        out_specs=pl.BlockSpec((tm, tn), lambda n, m: (m, n)),
        out_shape=jax.ShapeDtypeStruct((T, nrows), out_dtype),
        scratch_shapes=[pltpu.VMEM((tn, K), BF16)],
        compiler_params=_params("parallel", "arbitrary"), name="in_proj")
    if shift == 0:
        return pl.pallas_call(_proj_t_kernel, in_specs=[h_spec, w_spec], **common)(h, wt)
    assert shift % 16 == 0 or shift == 8, shift
    assert tn % shift == 0 and base % shift == 0
    per_tile = tn // shift
    wnext_spec = pl.BlockSpec((None, shift, K), lambda n, m: (layer, (r0 + n + 1) * per_tile, 0))
    return pl.pallas_call(functools.partial(_proj_t_shifted_kernel, shift=shift),
                          in_specs=[h_spec, w_spec, wnext_spec], **common)(h, wt, wt)


def _out_proj_kernel(hm_ref, ha_ref, w_ref, x_ref, o_ref, wbf_ref):
    @pl.when(pl.program_id(1) == 0)
    def _():
        wbf_ref[...] = w_ref[...].astype(BF16)

    km = hm_ref.shape[1]
    o_ref[...] = x_ref[...] + (_dot(hm_ref[...], wbf_ref[:km, :]) + _dot(ha_ref[...], wbf_ref[km:, :]))


def _out_proj(hm, ha, w, layer, x2, tm=1024, tn=1024):
    T, km = hm.shape
    ka = ha.shape[1]
    N = x2.shape[1]
    return pl.pallas_call(
        _out_proj_kernel, grid=(N // tn, T // tm),
        in_specs=[pl.BlockSpec((tm, km), lambda n, m: (m, 0)),
                  pl.BlockSpec((tm, ka), lambda n, m: (m, 0)),
                  pl.BlockSpec((None, km + ka, tn), lambda n, m: (layer, 0, n)),
                  pl.BlockSpec((tm, tn), lambda n, m: (m, n))],
        out_specs=pl.BlockSpec((tm, tn), lambda n, m: (m, n)),
        out_shape=jax.ShapeDtypeStruct((T, N), F32),
        scratch_shapes=[pltpu.VMEM((km + ka, tn), BF16)],
        compiler_params=_params("parallel", "arbitrary"), name="out_proj")(hm, ha, w, x2)


def _down_proj_kernel(a_ref, w_ref, x_ref, o_ref):
    o_ref[...] = x_ref[...] + _dot(a_ref[...], w_ref[...])


def _down_proj(a, w_bf, layer, x2, tm=512, tn=1024):
    T, K = a.shape
    N = x2.shape[1]
    return pl.pallas_call(
        _down_proj_kernel, grid=(N // tn, T // tm),
        in_specs=[pl.BlockSpec((tm, K), lambda n, m: (m, 0)),
                  pl.BlockSpec((None, K, tn), lambda n, m: (layer, 0, n)),
                  pl.BlockSpec((tm, tn), lambda n, m: (m, n))],
        out_specs=pl.BlockSpec((tm, tn), lambda n, m: (m, n)),
        out_shape=jax.ShapeDtypeStruct((T, N), F32),
        compiler_params=_params("parallel", "arbitrary"), name="down_proj")(a, w_bf, x2)


def _up_proj_kernel(h_ref, wg_ref, wv_ref, cwg_ref, cwv_ref, cbg_ref, cbv_ref, a_ref,
                    wgbf_ref, wvbf_ref, carry_g_ref, carry_v_ref, *, tiles_per_seq):
    m = pl.program_id(1)

    @pl.when(m == 0)
    def _():
        wgbf_ref[...] = wg_ref[...].astype(BF16)
        wvbf_ref[...] = wv_ref[...].astype(BF16)

    @pl.when(m % tiles_per_seq == 0)
    def _():
        carry_g_ref[...] = jnp.zeros_like(carry_g_ref)
        carry_v_ref[...] = jnp.zeros_like(carry_v_ref)

    h = h_ref[...]
    tm = h.shape[0]
    row = lax.broadcasted_iota(jnp.int32, (tm, 1), 0)

    def shifted(u, tail, k):
        r = pltpu.roll(u, k, 0)
        top = jnp.where(row[:HALO] < k, pltpu.roll(tail, k, 0), r[:HALO])
        return jnp.concatenate([top, r[HALO:]], axis=0)

    def conv(u, cw_ref, cb_ref, carry_ref, cols):
        tail = carry_ref[:, cols]
        carry_ref[:, cols] = u[tm - HALO:, :]
        cw = cw_ref[:, cols]
        return ((cb_ref[:, cols] + cw[0:1, :] * shifted(u, tail, 2)) + cw[1:2, :] * shifted(u, tail, 1)) \
            + cw[2:3, :] * u

    def dot_rows(w):
        return jnp.concatenate([_dot(h[r0:r0 + UP_ROWS], w) for r0 in range(0, tm, UP_ROWS)], axis=0)

    for c0 in range(0, a_ref.shape[1], UP_COLS):
        cols = slice(c0, c0 + UP_COLS)
        yg = conv(dot_rows(wgbf_ref[:, cols]), cwg_ref, cbg_ref, carry_g_ref, cols)
        yv = conv(dot_rows(wvbf_ref[:, cols]), cwv_ref, cbv_ref, carry_v_ref, cols)
        a_ref[:, cols] = (yg * _sigmoid(yg) * yv).astype(BF16)


def _up_proj(h, w, layer, cw, cb, seq_len, tm=512, tf=512):
    T, K = h.shape
    d_ff = w.shape[2] // 2
    nf = d_ff // tf
    assert d_ff % tf == 0 and seq_len % tm == 0
    kern = functools.partial(_up_proj_kernel, tiles_per_seq=seq_len // tm)
    return pl.pallas_call(
        kern, grid=(nf, T // tm),
        in_specs=[pl.BlockSpec((tm, K), lambda n, m: (m, 0)),
                  pl.BlockSpec((None, K, tf), lambda n, m: (layer, 0, n)),
                  pl.BlockSpec((None, K, tf), lambda n, m: (layer, 0, nf + n)),
                  pl.BlockSpec((FFN_CONV, tf), lambda n, m: (0, n)),
                  pl.BlockSpec((FFN_CONV, tf), lambda n, m: (0, nf + n)),
                  pl.BlockSpec((1, tf), lambda n, m: (0, n)),
                  pl.BlockSpec((1, tf), lambda n, m: (0, nf + n))],
        out_specs=pl.BlockSpec((tm, tf), lambda n, m: (m, n)),
        out_shape=jax.ShapeDtypeStruct((T, d_ff), BF16),
        scratch_shapes=[pltpu.VMEM((K, tf), BF16), pltpu.VMEM((K, tf), BF16),
                        pltpu.VMEM((HALO, tf), F32), pltpu.VMEM((HALO, tf), F32)],
        compiler_params=_params("parallel", "arbitrary"), name="up_proj")(
            h, w, w, cw, cw, cb.reshape(1, -1), cb.reshape(1, -1))


def _mlstm_kernel(q_ref, k_ref, v_ref, o_ref, g_ref, gb_ref, cwq_ref, cwk_ref, cbq_ref, cbk_ref, nw_ref,
                  out_ref, pad_ref, qs_ref, ks_ref, sv_ref, kv_ref, *, chunk):
    head = pl.program_id(1)
    S, dh = q_ref.shape[1], q_ref.shape[2]
    L = chunk
    nc = S // L
    halo = 8

    def conv_silu(x_ref, cw_ref, cb_ref, dst_ref, scale):
        pad_ref[0:halo, :] = jnp.zeros((halo, dh), F32)
        pad_ref[halo:halo + S, :] = x_ref[0]
        cw = cw_ref[...]
        cb = cb_ref[...]
        for c in range(nc):
            y = cb
            for j in range(MLSTM_CONV):
                off = halo + c * L - (MLSTM_CONV - 1) + j
                y = y + cw[j:j + 1, :] * pad_ref[off:off + L, :]
            dst_ref[c * L:(c + 1) * L, :] = (y * _sigmoid(y) * scale).astype(BF16)

    conv_silu(q_ref, cwq_ref, cbq_ref, qs_ref, dh ** -0.5)
    conv_silu(k_ref, cwk_ref, cbk_ref, ks_ref, 1.0)

    rr = lax.broadcasted_iota(jnp.int32, (L, L), 0)
    cc = lax.broadcasted_iota(jnp.int32, (L, L), 1)
    tri = rr >= cc
    eye = rr == cc
    tri_bf = jnp.where(tri, 1.0, 0.0).astype(BF16)
    lane = lax.broadcasted_iota(jnp.int32, (L, LANES), 1)

    def lane_pick(x, idx):
        return jnp.sum(jnp.where(lane == idx, x, 0.0), axis=1, keepdims=True)

    def local(c):
        sl = slice(c * L, (c + 1) * L)
        G = g_ref[0, sl, :] + gb_ref[...]
        li = lane_pick(G, head)
        LF = jnp.minimum(G, 0.0) - jnp.log1p(jnp.exp(-jnp.abs(G)))
        hi, mid, lo = _split3(LF)
        b = lane_pick(_dot(tri_bf, hi) + _dot(tri_bf, mid) + _dot(tri_bf, lo), MLSTM_HEADS + head)
        r_row = jnp.sum(jnp.where(eye, li - b, 0.0), axis=0, keepdims=True)
        Dm = jnp.where(tri, b + r_row, -jnp.inf)
        mrow = jnp.max(Dm, axis=1, keepdims=True)
        qc, kc, vc = qs_ref[sl, :], ks_ref[sl, :], v_ref[0, sl, :]
        s0 = _dot_nt(qc, kc) * jnp.exp(Dm - mrow)
        sv_ref[c] = _dot(s0.astype(BF16), vc.astype(BF16))
        bL = b[L - 1:L, :]
        gcol = bL - b + li
        gmax = jnp.max(gcol, axis=0, keepdims=True)
        wk0 = jnp.exp(gcol - gmax)
        kv_ref[c] = _dot_tn(kc, (wk0 * vc).astype(BF16))
        ksum = jnp.sum(wk0 * kc.astype(F32), axis=0, keepdims=True)
        return b, mrow, jnp.sum(s0, axis=1, keepdims=True), bL, gmax, ksum

    C = jnp.zeros((dh, dh), F32)
    n = jnp.zeros((1, dh), F32)
    m = jnp.zeros((1, 1), F32)
    for c in range(nc):
        sl = slice(c * L, (c + 1) * L)
        b, mrow, rs0, bL, gmax, ksum = local(c)
        inter = b + m
        m_t = jnp.maximum(inter, mrow)
        w_inter = jnp.exp(inter - m_t)
        f = jnp.exp(mrow - m_t)
        qc = qs_ref[sl, :]
        num = w_inter * _dot(qc, C.astype(BF16)) + f * sv_ref[c]
        den = w_inter * jnp.sum(qc.astype(F32) * n, axis=1, keepdims=True) + f * rs0
        hraw = num * (1.0 / jnp.maximum(jnp.abs(den), jnp.exp(-m_t)))
        y = _rms(hraw, nw_ref[...])
        out_ref[0, sl, :] = (y * _sigmoid(o_ref[0, sl, :])).astype(BF16)

        m_new = jnp.maximum(bL + m, gmax)
        a = jnp.exp(bL + m - m_new)
        fk = jnp.exp(gmax - m_new)
        C = a * C + fk * kv_ref[c]
        n = a * n + fk * ksum
        m = m_new


def _mlstm(zA, gates, gate_bias, conv_w, conv_b, norm_w, chunk=MLSTM_CHUNK):
    B, S, four_dm = zA.shape
    dm = four_dm // 4
    H = MLSTM_HEADS
    dh = dm // H
    kern = functools.partial(_mlstm_kernel, chunk=chunk)

    def col(base):
        return pl.BlockSpec((1, S, dh), lambda b, h: (b, 0, base + h))

    return pl.pallas_call(
        kern, grid=(B, H),
        in_specs=[col(0), col(H), col(2 * H), col(3 * H),
                  pl.BlockSpec((1, S, LANES), lambda b, h: (b, 0, 0)),
                  pl.BlockSpec((1, LANES), lambda b, h: (0, 0)),
                  pl.BlockSpec((MLSTM_CONV, dh), lambda b, h: (0, h)),
                  pl.BlockSpec((MLSTM_CONV, dh), lambda b, h: (0, H + h)),
                  pl.BlockSpec((1, dh), lambda b, h: (0, h)),
                  pl.BlockSpec((1, dh), lambda b, h: (0, H + h)),
                  pl.BlockSpec((None, 1, dh), lambda b, h: (h, 0, 0))],
        out_specs=pl.BlockSpec((1, S, dh), lambda b, h: (b, 0, h)),
        out_shape=jax.ShapeDtypeStruct((B, S, dm), BF16),
        scratch_shapes=[pltpu.VMEM((S + 8, dh), F32), pltpu.VMEM((S, dh), BF16), pltpu.VMEM((S, dh), BF16),
                        pltpu.VMEM((S // chunk, chunk, dh), F32), pltpu.VMEM((S // chunk, dh, dh), F32)],
        compiler_params=_params("parallel", "parallel"), name="mlstm")(
            zA, zA, zA, zA, gates, gate_bias, conv_w, conv_w, conv_b.reshape(1, -1), conv_b.reshape(1, -1),
            norm_w.reshape(H, 1, dh))


def _moba_kernel(rb_ref, q_ref, k_ref, v_ref, qkn_ref, out_ref,
                 bias_ref, sc_ref, qa_ref, ka_ref, va_ref):
    head = pl.program_id(0)
    S, dh = q_ref.shape[1], q_ref.shape[2]
    BLK = MOBA_BLOCK
    NB = S // BLK
    scale = dh ** -0.5

    @pl.when(pl.program_id(1) == 0)
    def _build_bias():
        rr = lax.broadcasted_iota(jnp.int32, (BLK, BLK), 0)
        cc = lax.broadcasted_iota(jnp.int32, (BLK, BLK), 1)
        for delta in range(NB):
            d = (rr - cc) + delta * BLK
            lo, hi = max(delta * BLK - (BLK - 1), 0), delta * BLK + BLK - 1
            tile = jnp.full((BLK, BLK), rb_ref[int(_T5_BUCKET[lo]), head], F32)
            for kb in range(int(_T5_BUCKET[lo]) + 1, int(_T5_BUCKET[hi]) + 1):
                tile = jnp.where(d >= _T5_THR[kb], rb_ref[kb, head], tile)
            if delta == 0:
                tile = jnp.where(d >= 0, tile, -jnp.inf)
            bias_ref[delta] = tile

    qn = _rms(q_ref[0], qkn_ref[0:1, :])
    kn = _rms(k_ref[0], qkn_ref[1:2, :])
    q_hi = qn.astype(BF16)
    qa_ref[:, :, :dh] = q_hi.reshape(NB, BLK, dh)
    lane_k = lax.broadcasted_iota(jnp.int32, (BLK, dh), 1)
    ka_ref[:, :, :dh] = kn.astype(BF16).reshape(NB, BLK, dh)
    for j in range(NB):
        ka_ref[j, :, dh:] = jnp.where(lane_k == j, 1.0, 0.0).astype(BF16)
    va_ref[:, :, :dh] = v_ref[0].astype(BF16).reshape(NB, BLK, dh)
    va_ref[:, :, dh:] = jnp.ones((NB, BLK, dh), BF16)

    kmean = jnp.mean(kn.reshape(NB, BLK, dh), axis=1)
    kmean = jnp.concatenate([kmean, jnp.zeros((16 - NB, dh), F32)], axis=0)
    q_lo = (qn - q_hi.astype(F32)).astype(BF16)
    m_hi = kmean.astype(BF16)
    m_lo = (kmean - m_hi.astype(F32)).astype(BF16)
    gate_t = _dot_nt(m_hi, q_hi) + (_dot_nt(m_lo, q_hi) + _dot_nt(m_hi, q_lo))

    blk = lax.broadcasted_iota(jnp.int32, (NB, BLK), 0)
    for i in range(1, NB):
        g = gate_t[:NB, i * BLK:(i + 1) * BLK]
        cnt = jnp.zeros((NB, BLK), F32)
        for jp in range(i):
            gj = g[jp:jp + 1, :]
            beats = (gj > g) | ((gj == g) & (blk > jp))
            cnt = cnt + jnp.where(beats, 1.0, 0.0)
        mask_t = jnp.where((cnt < MOBA_TOPK) | (blk >= i), 0.0, MASKED)
        mask_t = jnp.concatenate([mask_t, jnp.zeros((dh - NB, BLK), F32)], axis=0)
        qa_ref[i, :, dh:] = mask_t.T.astype(BF16)
    qa_ref[0, :, dh:] = jnp.zeros((BLK, dh), BF16)

    tile = lambda i, j: i * (i + 1) // 2 + j
    mxs = []
    for i in range(NB):
        qa = qa_ref[i]
        mx = None
        for j in range(i + 1):
            s = _dot_nt(qa, ka_ref[j]) * scale + bias_ref[i - j]
            sc_ref[tile(i, j)] = s
            mj = jnp.maximum(s[:, :LANES], s[:, LANES:])
            mx = mj if mx is None else jnp.maximum(mx, mj)
        mxs.append(jnp.max(mx, axis=1, keepdims=True))

    for i in range(NB):
        acc = jnp.zeros((BLK, 2 * dh), F32)
        for j in range(i + 1):
            acc = acc + _dot(jnp.exp(sc_ref[tile(i, j)] - mxs[i]).astype(BF16), va_ref[j])
        out_ref[0, i * BLK:(i + 1) * BLK, :] = (acc[:, :dh] * (1.0 / acc[:, dh:dh + 1])).astype(BF16)


def _moba(zB, qk_norm_l, rel_bias):
    B, S, three_da = zB.shape
    da = three_da // 3
    H = MOBA_HEADS
    dh = da // H
    NB = S // MOBA_BLOCK
    BLK = MOBA_BLOCK

    def col(base):
        return pl.BlockSpec((1, S, dh), lambda h, b: (b, 0, base + h))

    return pl.pallas_call(
        _moba_kernel, grid=(H, B),
        in_specs=[pl.BlockSpec(memory_space=pltpu.SMEM),
                  col(0), col(H), col(2 * H),
                  pl.BlockSpec((2, dh), lambda h, b: (0, 0))],
        out_specs=pl.BlockSpec((1, S, dh), lambda h, b: (b, 0, h)),
        out_shape=jax.ShapeDtypeStruct((B, S, da), BF16),
        scratch_shapes=[pltpu.VMEM((NB, BLK, BLK), F32), pltpu.VMEM((NB * (NB + 1) // 2, BLK, BLK), F32),
                        pltpu.VMEM((NB, BLK, 2 * dh), BF16), pltpu.VMEM((NB, BLK, 2 * dh), BF16),
                        pltpu.VMEM((NB, BLK, 2 * dh), BF16)],
        compiler_params=_params("parallel", "arbitrary"), name="moba")(rel_bias, zB, zB, zB, qk_norm_l)


def kernel(x, norm_mix, w_in, gate_bias, conv_qk_w, conv_qk_b, mlstm_norm, qk_norm, rel_bias, w_out, norm_ffn,
           w_up, conv_ffn_w, conv_ffn_b, w_down):
    B, S, D = x.shape
    depth = w_in.shape[0]
    H = MLSTM_HEADS
    dm = mlstm_norm.shape[1] * mlstm_norm.shape[2]
    da = w_out.shape[1] - dm
    n_gate = 2 * H
    g0 = 4 * dm
    assert w_in.shape[2] == g0 + n_gate + 3 * da

    gb = jnp.pad(gate_bias, ((0, 0), (0, LANES - n_gate)))

    w_in_t = jnp.swapaxes(w_in, 1, 2)
    w_down_bf = w_down.astype(BF16)

    x2 = x.reshape(B * S, D)
    for l in range(depth):
        h, gates = _norm(x2, norm_mix[l], w_in_t, l, g0)
        zA = _proj_t(h, w_in_t, l, 0, g0)
        zB = _proj_t(h, w_in_t, l, g0 + n_gate, 3 * da)
        hm = _mlstm(zA.reshape(B, S, g0), gates.reshape(B, S, LANES), gb[l:l + 1], conv_qk_w[l], conv_qk_b[l],
                    mlstm_norm[l])
        ha = _moba(zB.reshape(B, S, 3 * da), qk_norm[l], rel_bias)
        x2 = _out_proj(hm.reshape(B * S, dm), ha.reshape(B * S, da), w_out, l, x2)
        h2 = _norm(x2, norm_ffn[l])
        a = _up_proj(h2, w_up, l, conv_ffn_w[l], conv_ffn_b[l], S)
        x2 = _down_proj(a, w_down_bf, l, x2)
    return x2.reshape(B, S, D)
```

```python
import functools
import math

import numpy as np
import jax
import jax.numpy as jnp
from jax import lax
from jax.experimental import pallas as pl
from jax.experimental.pallas import tpu as pltpu

F32 = jnp.float32
BF16 = jnp.bfloat16

EPS = 1e-6
MLSTM_HEADS = 4
MLSTM_CONV = 4
MLSTM_CHUNK = 256
MOBA_HEADS = 8
MOBA_BLOCK = 256
MOBA_TOPK = 3
MASKED = -1e30
N_BUCKETS = 32
MAX_DISTANCE = 2048
FFN_CONV = 3
HALO = 8
UP_ROWS = 256
UP_COLS = 256

LANES = 128
VMEM_LIMIT = 56 * 1024 * 1024


def _t5_thresholds():
    max_exact = N_BUCKETS // 2
    d = np.arange(0, MAX_DISTANCE, dtype=np.float64)
    ratio = np.maximum(d, max_exact) / max_exact
    large = max_exact + (np.log(ratio) / math.log(MAX_DISTANCE / max_exact)
                         * (N_BUCKETS - max_exact)).astype(np.int64)
    bucket = np.where(d < max_exact, d.astype(np.int64), np.minimum(large, N_BUCKETS - 1))
    assert np.all(np.diff(bucket) >= 0)
    thr = [int(np.argmax(bucket >= k)) if np.any(bucket >= k) else MAX_DISTANCE for k in range(N_BUCKETS)]
    return bucket, thr


_T5_BUCKET, _T5_THR = _t5_thresholds()


def _sigmoid(x):
    return 1.0 / (1.0 + jnp.exp(-x))


def _split3(x):
    hi = x.astype(BF16)
    r1 = x - hi.astype(F32)
    mid = r1.astype(BF16)
    lo = (r1 - mid.astype(F32)).astype(BF16)
    return hi, mid, lo


def _dot(a, b):
    return jnp.dot(a, b, preferred_element_type=F32)


def _dot_nt(a, b):
    return lax.dot_general(a, b, (((1,), (1,)), ((), ())), preferred_element_type=F32)


def _dot_tn(a, b):
    return lax.dot_general(a, b, (((0,), (0,)), ((), ())), preferred_element_type=F32)


def _params(*sem, flags=None):
    return pltpu.CompilerParams(dimension_semantics=sem, vmem_limit_bytes=VMEM_LIMIT, flags=flags)


def _rms(x, g):
    return x * lax.rsqrt(jnp.mean(x * x, axis=-1, keepdims=True) + EPS) * g


def _norm_kernel(x_ref, g_ref, h_ref):
    h_ref[...] = _rms(x_ref[...], g_ref[...]).astype(BF16)


def _norm_gates_kernel(x_ref, g_ref, wg_ref, h_ref, gates_ref):
    h = _rms(x_ref[...], g_ref[...]).astype(BF16)
    h_ref[...] = h
    gates_ref[...] = _dot_nt(h, wg_ref[...].astype(BF16))


def _norm(x2, g, wt=None, layer=0, gate_row=None, tm=1024):
    T, D = x2.shape
    x_spec = pl.BlockSpec((tm, D), lambda i: (i, 0))
    g_spec = pl.BlockSpec((1, D), lambda i: (0, 0))
    h_spec = pl.BlockSpec((tm, D), lambda i: (i, 0))
    if wt is None:
        return pl.pallas_call(
            _norm_kernel, grid=(T // tm,), in_specs=[x_spec, g_spec], out_specs=h_spec,
            out_shape=jax.ShapeDtypeStruct((T, D), BF16), compiler_params=_params("parallel"),
            name="rmsnorm")(x2, g.reshape(1, D))
    assert gate_row % LANES == 0
    return pl.pallas_call(
        _norm_gates_kernel, grid=(T // tm,),
        in_specs=[x_spec, g_spec, pl.BlockSpec((None, LANES, D), lambda i: (layer, gate_row // LANES, 0))],
        out_specs=[h_spec, pl.BlockSpec((tm, LANES), lambda i: (i, 0))],
        out_shape=[jax.ShapeDtypeStruct((T, D), BF16), jax.ShapeDtypeStruct((T, LANES), F32)],
        compiler_params=_params("parallel"), name="rmsnorm_gates")(x2, g.reshape(1, D), wt)


def _proj_t_kernel(h_ref, w_ref, o_ref, wbf_ref):
    @pl.when(pl.program_id(1) == 0)
    def _():
        wbf_ref[...] = w_ref[...].astype(BF16)

    o_ref[...] = _dot_nt(h_ref[...], wbf_ref[...]).astype(o_ref.dtype)


def _proj_t_shifted_kernel(h_ref, w_ref, wnext_ref, o_ref, wbf_ref, *, shift):
    @pl.when(pl.program_id(1) == 0)
    def _():
        wbf_ref[...] = jnp.concatenate([w_ref[shift:, :], wnext_ref[...]], axis=0).astype(BF16)

    o_ref[...] = _dot_nt(h_ref[...], wbf_ref[...]).astype(o_ref.dtype)


def _proj_t(h, wt, layer, row0, nrows, tm=1024, tn=1024, out_dtype=F32):
    T, K = h.shape
    shift = row0 % tn
    base = row0 - shift
    assert nrows % tn == 0 and T % tm == 0
    r0 = base // tn
    h_spec = pl.BlockSpec((tm, K), lambda n, m: (m, 0))
    w_spec = pl.BlockSpec((None, tn, K), lambda n, m: (layer, r0 + n, 0))
    common = dict(
        grid=(nrows // tn, T // tm),
        out_specs=pl.BlockSpec((tm, tn), lambda n, m: (m, n)),
        out_shape=jax.ShapeDtypeStruct((T, nrows), out_dtype),
        scratch_shapes=[pltpu.VMEM((tn, K), BF16)],
        compiler_params=_params("parallel", "arbitrary"), name="in_proj")
    if shift == 0:
        return pl.pallas_call(_proj_t_kernel, in_specs=[h_spec, w_spec], **common)(h, wt)
    assert shift % 16 == 0 or shift == 8, shift
    assert tn % shift == 0 and base % shift == 0
    per_tile = tn // shift
    wnext_spec = pl.BlockSpec((None, shift, K), lambda n, m: (layer, (r0 + n + 1) * per_tile, 0))
    return pl.pallas_call(functools.partial(_proj_t_shifted_kernel, shift=shift),
                          in_specs=[h_spec, w_spec, wnext_spec], **common)(h, wt, wt)


def _out_proj_kernel(hm_ref, ha_ref, w_ref, x_ref, o_ref, wbf_ref):
    @pl.when(pl.program_id(1) == 0)
    def _():
        wbf_ref[...] = w_ref[...].astype(BF16)

    km = hm_ref.shape[1]
    o_ref[...] = x_ref[...] + (_dot(hm_ref[...], wbf_ref[:km, :]) + _dot(ha_ref[...], wbf_ref[km:, :]))


def _out_proj(hm, ha, w, layer, x2, tm=1024, tn=1024):
    T, km = hm.shape
    ka = ha.shape[1]
    N = x2.shape[1]
    return pl.pallas_call(
        _out_proj_kernel, grid=(N // tn, T // tm),
        in_specs=[pl.BlockSpec((tm, km), lambda n, m: (m, 0)),
                  pl.BlockSpec((tm, ka), lambda n, m: (m, 0)),
                  pl.BlockSpec((None, km + ka, tn), lambda n, m: (layer, 0, n)),
                  pl.BlockSpec((tm, tn), lambda n, m: (m, n))],
        out_specs=pl.BlockSpec((tm, tn), lambda n, m: (m, n)),
        out_shape=jax.ShapeDtypeStruct((T, N), F32),
        scratch_shapes=[pltpu.VMEM((km + ka, tn), BF16)],
        compiler_params=_params("parallel", "arbitrary"), name="out_proj")(hm, ha, w, x2)


def _down_proj_kernel(a_ref, w_ref, x_ref, o_ref):
    o_ref[...] = x_ref[...] + _dot(a_ref[...], w_ref[...])


def _down_proj(a, w_bf, layer, x2, tm=512, tn=1024):
    T, K = a.shape
    N = x2.shape[1]
    return pl.pallas_call(
        _down_proj_kernel, grid=(N // tn, T // tm),
        in_specs=[pl.BlockSpec((tm, K), lambda n, m: (m, 0)),
                  pl.BlockSpec((None, K, tn), lambda n, m: (layer, 0, n)),
                  pl.BlockSpec((tm, tn), lambda n, m: (m, n))],
        out_specs=pl.BlockSpec((tm, tn), lambda n, m: (m, n)),
        out_shape=jax.ShapeDtypeStruct((T, N), F32),
        compiler_params=_params("parallel", "arbitrary"), name="down_proj")(a, w_bf, x2)


def _up_proj_kernel(h_ref, wg_ref, wv_ref, cwg_ref, cwv_ref, cbg_ref, cbv_ref, a_ref,
                    wgbf_ref, wvbf_ref, carry_g_ref, carry_v_ref, *, tiles_per_seq):
    m = pl.program_id(1)

    @pl.when(m == 0)
    def _():
        wgbf_ref[...] = wg_ref[...].astype(BF16)
        wvbf_ref[...] = wv_ref[...].astype(BF16)

    @pl.when(m % tiles_per_seq == 0)
    def _():
        carry_g_ref[...] = jnp.zeros_like(carry_g_ref)
        carry_v_ref[...] = jnp.zeros_like(carry_v_ref)

    tm, tf = a_ref.shape
    row8 = lax.broadcasted_iota(jnp.int32, (HALO, 1), 0)
    cwg, cwv, cbg, cbv = cwg_ref[...], cwv_ref[...], cbg_ref[...], cbv_ref[...]

    def shifted(u, tail, k):
        r = pltpu.roll(u, k, 0)
        top = jnp.where(row8 < k, pltpu.roll(tail, k, 0), r[:HALO])
        return jnp.concatenate([top, r[HALO:]], axis=0)

    def conv(u, tail, cw, cb):
        return ((cb + cw[0:1, :] * shifted(u, tail, 2)) + cw[1:2, :] * shifted(u, tail, 1)) + cw[2:3, :] * u

    def dot_cols(hb, w_ref):
        return jnp.concatenate([_dot(hb, w_ref[:, c0:c0 + UP_COLS]) for c0 in range(0, tf, UP_COLS)], axis=1)

    tail_g, tail_v = carry_g_ref[...], carry_v_ref[...]
    for r0 in range(0, tm, UP_ROWS):
        hb = h_ref[r0:r0 + UP_ROWS, :]
        ug = dot_cols(hb, wgbf_ref)
        uv = dot_cols(hb, wvbf_ref)
        yg = conv(ug, tail_g, cwg, cbg)
        yv = conv(uv, tail_v, cwv, cbv)
        a_ref[r0:r0 + UP_ROWS, :] = (yg * _sigmoid(yg) * yv).astype(BF16)
        tail_g, tail_v = ug[UP_ROWS - HALO:, :], uv[UP_ROWS - HALO:, :]
    carry_g_ref[...] = tail_g
    carry_v_ref[...] = tail_v


def _up_proj(h, w, layer, cw, cb, seq_len, tm=1024, tf=512):
    T, K = h.shape
    d_ff = w.shape[2] // 2
    nf = d_ff // tf
    assert d_ff % tf == 0 and seq_len % tm == 0
    kern = functools.partial(_up_proj_kernel, tiles_per_seq=seq_len // tm)
    return pl.pallas_call(
        kern, grid=(nf, T // tm),
        in_specs=[pl.BlockSpec((tm, K), lambda n, m: (m, 0)),
                  pl.BlockSpec((None, K, tf), lambda n, m: (layer, 0, n)),
                  pl.BlockSpec((None, K, tf), lambda n, m: (layer, 0, nf + n)),
                  pl.BlockSpec((FFN_CONV, tf), lambda n, m: (0, n)),
                  pl.BlockSpec((FFN_CONV, tf), lambda n, m: (0, nf + n)),
                  pl.BlockSpec((1, tf), lambda n, m: (0, n)),
                  pl.BlockSpec((1, tf), lambda n, m: (0, nf + n))],
        out_specs=pl.BlockSpec((tm, tf), lambda n, m: (m, n)),
        out_shape=jax.ShapeDtypeStruct((T, d_ff), BF16),
        scratch_shapes=[pltpu.VMEM((K, tf), BF16), pltpu.VMEM((K, tf), BF16),
                        pltpu.VMEM((HALO, tf), F32), pltpu.VMEM((HALO, tf), F32)],
        compiler_params=_params("parallel", "arbitrary"), name="up_proj")(
            h, w, w, cw, cw, cb.reshape(1, -1), cb.reshape(1, -1))


def _mlstm_kernel(q_ref, k_ref, v_ref, o_ref, g_ref, gb_ref, cwq_ref, cwk_ref, cbq_ref, cbk_ref, nw_ref,
                  out_ref, pad_ref, qs_ref, ks_ref, sv_ref, kv_ref, *, chunk):
    head = pl.program_id(1)
    S, dh = q_ref.shape[1], q_ref.shape[2]
    L = chunk
    nc = S // L
    halo = 8

    def conv_silu(x_ref, cw_ref, cb_ref, dst_ref, scale):
        pad_ref[0:halo, :] = jnp.zeros((halo, dh), F32)
        pad_ref[halo:halo + S, :] = x_ref[0]
        cw = cw_ref[...]
        cb = cb_ref[...]
        for c in range(nc):
            y = cb
            for j in range(MLSTM_CONV):
                off = halo + c * L - (MLSTM_CONV - 1) + j
                y = y + cw[j:j + 1, :] * pad_ref[off:off + L, :]
            dst_ref[c * L:(c + 1) * L, :] = (y * _sigmoid(y) * scale).astype(BF16)

    conv_silu(q_ref, cwq_ref, cbq_ref, qs_ref, dh ** -0.5)
    conv_silu(k_ref, cwk_ref, cbk_ref, ks_ref, 1.0)

    rr = lax.broadcasted_iota(jnp.int32, (L, L), 0)
    cc = lax.broadcasted_iota(jnp.int32, (L, L), 1)
    tri = rr >= cc
    eye = rr == cc
    tri_bf = jnp.where(tri, 1.0, 0.0).astype(BF16)
    lane = lax.broadcasted_iota(jnp.int32, (L, LANES), 1)

    def lane_pick(x, idx):
        return jnp.sum(jnp.where(lane == idx, x, 0.0), axis=1, keepdims=True)

    def local(c):
        sl = slice(c * L, (c + 1) * L)
        G = g_ref[0, sl, :] + gb_ref[...]
        li = lane_pick(G, head)
        LF = jnp.minimum(G, 0.0) - jnp.log1p(jnp.exp(-jnp.abs(G)))
        hi, mid, lo = _split3(LF)
        b = lane_pick(_dot(tri_bf, hi) + _dot(tri_bf, mid) + _dot(tri_bf, lo), MLSTM_HEADS + head)
        r_row = jnp.sum(jnp.where(eye, li - b, 0.0), axis=0, keepdims=True)
        Dm = jnp.where(tri, b + r_row, -jnp.inf)
        mrow = jnp.max(Dm, axis=1, keepdims=True)
        qc, kc, vc = qs_ref[sl, :], ks_ref[sl, :], v_ref[0, sl, :]
        s0 = _dot_nt(qc, kc) * jnp.exp(Dm - mrow)
        sv_ref[c] = _dot(s0.astype(BF16), vc.astype(BF16))
        bL = b[L - 1:L, :]
        gcol = bL - b + li
        gmax = jnp.max(gcol, axis=0, keepdims=True)
        wk0 = jnp.exp(gcol - gmax)
        kv_ref[c] = _dot_tn(kc, (wk0 * vc).astype(BF16))
        ksum = jnp.sum(wk0 * kc.astype(F32), axis=0, keepdims=True)
        return b, mrow, jnp.sum(s0, axis=1, keepdims=True), bL, gmax, ksum

    C = jnp.zeros((dh, dh), F32)
    n = jnp.zeros((1, dh), F32)
    m = jnp.zeros((1, 1), F32)
    for c in range(nc):
        sl = slice(c * L, (c + 1) * L)
        b, mrow, rs0, bL, gmax, ksum = local(c)
        inter = b + m
        m_t = jnp.maximum(inter, mrow)
        w_inter = jnp.exp(inter - m_t)
        f = jnp.exp(mrow - m_t)
        qc = qs_ref[sl, :]
        num = w_inter * _dot(qc, C.astype(BF16)) + f * sv_ref[c]
        den = w_inter * jnp.sum(qc.astype(F32) * n, axis=1, keepdims=True) + f * rs0
        hraw = num * (1.0 / jnp.maximum(jnp.abs(den), jnp.exp(-m_t)))
        y = _rms(hraw, nw_ref[...])
        out_ref[0, sl, :] = (y * _sigmoid(o_ref[0, sl, :])).astype(BF16)

        m_new = jnp.maximum(bL + m, gmax)
        a = jnp.exp(bL + m - m_new)
        fk = jnp.exp(gmax - m_new)
        C = a * C + fk * kv_ref[c]
        n = a * n + fk * ksum
        m = m_new


def _mlstm(zA, gates, gate_bias, conv_w, conv_b, norm_w, chunk=MLSTM_CHUNK):
    B, S, four_dm = zA.shape
    dm = four_dm // 4
    H = MLSTM_HEADS
    dh = dm // H
    kern = functools.partial(_mlstm_kernel, chunk=chunk)

    def col(base):
        return pl.BlockSpec((1, S, dh), lambda b, h: (b, 0, base + h))

    return pl.pallas_call(
        kern, grid=(B, H),
        in_specs=[col(0), col(H), col(2 * H), col(3 * H),
                  pl.BlockSpec((1, S, LANES), lambda b, h: (b, 0, 0)),
                  pl.BlockSpec((1, LANES), lambda b, h: (0, 0)),
                  pl.BlockSpec((MLSTM_CONV, dh), lambda b, h: (0, h)),
                  pl.BlockSpec((MLSTM_CONV, dh), lambda b, h: (0, H + h)),
                  pl.BlockSpec((1, dh), lambda b, h: (0, h)),
                  pl.BlockSpec((1, dh), lambda b, h: (0, H + h)),
                  pl.BlockSpec((None, 1, dh), lambda b, h: (h, 0, 0))],
        out_specs=pl.BlockSpec((1, S, dh), lambda b, h: (b, 0, h)),
        out_shape=jax.ShapeDtypeStruct((B, S, dm), BF16),
        scratch_shapes=[pltpu.VMEM((S + 8, dh), F32), pltpu.VMEM((S, dh), BF16), pltpu.VMEM((S, dh), BF16),
                        pltpu.VMEM((S // chunk, chunk, dh), F32), pltpu.VMEM((S // chunk, dh, dh), F32)],
        compiler_params=_params("parallel", "parallel"), name="mlstm")(
            zA, zA, zA, zA, gates, gate_bias, conv_w, conv_w, conv_b.reshape(1, -1), conv_b.reshape(1, -1),
            norm_w.reshape(H, 1, dh))


def _moba_kernel(rb_ref, q_ref, k_ref, v_ref, qkn_ref, out_ref,
                 bias_ref, sc_ref, qa_ref, ka_ref, va_ref):
    head = pl.program_id(0)
    S, dh = q_ref.shape[1], q_ref.shape[2]
    BLK = MOBA_BLOCK
    NB = S // BLK
    scale = dh ** -0.5

    @pl.when(pl.program_id(1) == 0)
    def _build_bias():
        rr = lax.broadcasted_iota(jnp.int32, (BLK, BLK), 0)
        cc = lax.broadcasted_iota(jnp.int32, (BLK, BLK), 1)
        for delta in range(NB):
            d = (rr - cc) + delta * BLK
            lo, hi = max(delta * BLK - (BLK - 1), 0), delta * BLK + BLK - 1
            tile = jnp.full((BLK, BLK), rb_ref[int(_T5_BUCKET[lo]), head], F32)
            for kb in range(int(_T5_BUCKET[lo]) + 1, int(_T5_BUCKET[hi]) + 1):
                tile = jnp.where(d >= _T5_THR[kb], rb_ref[kb, head], tile)
            if delta == 0:
                tile = jnp.where(d >= 0, tile, -jnp.inf)
            bias_ref[delta] = tile

    qn = _rms(q_ref[0], qkn_ref[0:1, :])
    kn = _rms(k_ref[0], qkn_ref[1:2, :])
    q_hi = qn.astype(BF16)
    qa_ref[:, :, :dh] = q_hi.reshape(NB, BLK, dh)
    lane_k = lax.broadcasted_iota(jnp.int32, (BLK, dh), 1)
    ka_ref[:, :, :dh] = kn.astype(BF16).reshape(NB, BLK, dh)
    for j in range(NB):
        ka_ref[j, :, dh:] = jnp.where(lane_k == j, 1.0, 0.0).astype(BF16)
    va_ref[:, :, :dh] = v_ref[0].astype(BF16).reshape(NB, BLK, dh)
    va_ref[:, :, dh:] = jnp.ones((NB, BLK, dh), BF16)

    kmean = jnp.mean(kn.reshape(NB, BLK, dh), axis=1)
    kmean = jnp.concatenate([kmean, jnp.zeros((16 - NB, dh), F32)], axis=0)
    q_lo = (qn - q_hi.astype(F32)).astype(BF16)
    m_hi = kmean.astype(BF16)
    m_lo = (kmean - m_hi.astype(F32)).astype(BF16)
    gate_t = _dot_nt(m_hi, q_hi) + (_dot_nt(m_lo, q_hi) + _dot_nt(m_hi, q_lo))

    blk = lax.broadcasted_iota(jnp.int32, (NB, BLK), 0)
    for i in range(1, NB):
        g = gate_t[:NB, i * BLK:(i + 1) * BLK]
        cnt = jnp.zeros((NB, BLK), F32)
        for jp in range(i):
            gj = g[jp:jp + 1, :]
            beats = (gj > g) | ((gj == g) & (blk > jp))
            cnt = cnt + jnp.where(beats, 1.0, 0.0)
        mask_t = jnp.where((cnt < MOBA_TOPK) | (blk >= i), 0.0, MASKED)
        mask_t = jnp.concatenate([mask_t, jnp.zeros((dh - NB, BLK), F32)], axis=0)
        qa_ref[i, :, dh:] = mask_t.T.astype(BF16)
    qa_ref[0, :, dh:] = jnp.zeros((BLK, dh), BF16)

    tile = lambda i, j: i * (i + 1) // 2 + j

    def scores(i):
        qa = qa_ref[i]
        mx = None
        for j in range(i + 1):
            s = _dot_nt(qa, ka_ref[j]) * scale + bias_ref[i - j]
            sc_ref[tile(i, j)] = s
            mj = jnp.maximum(s[:, :LANES], s[:, LANES:])
            mx = mj if mx is None else jnp.maximum(mx, mj)
        return jnp.max(mx, axis=1, keepdims=True)

    def attend(i, mx):
        acc = jnp.zeros((BLK, 2 * dh), F32)
        for j in range(i + 1):
            acc = acc + _dot(jnp.exp(sc_ref[tile(i, j)] - mx).astype(BF16), va_ref[j])
        out_ref[0, i * BLK:(i + 1) * BLK, :] = (acc[:, :dh] * (1.0 / acc[:, dh:dh + 1])).astype(BF16)

    mx_prev = scores(0)
    for i in range(1, NB):
        mx_next = scores(i)
        attend(i - 1, mx_prev)
        mx_prev = mx_next
    attend(NB - 1, mx_prev)


def _moba(zB, qk_norm_l, rel_bias):
    B, S, three_da = zB.shape
    da = three_da // 3
    H = MOBA_HEADS
    dh = da // H
    NB = S // MOBA_BLOCK
    BLK = MOBA_BLOCK

    def col(base):
        return pl.BlockSpec((1, S, dh), lambda h, b: (b, 0, base + h))

    return pl.pallas_call(
        _moba_kernel, grid=(H, B),
        in_specs=[pl.BlockSpec(memory_space=pltpu.SMEM),
                  col(0), col(H), col(2 * H),
                  pl.BlockSpec((2, dh), lambda h, b: (0, 0))],
        out_specs=pl.BlockSpec((1, S, dh), lambda h, b: (b, 0, h)),
        out_shape=jax.ShapeDtypeStruct((B, S, da), BF16),
        scratch_shapes=[pltpu.VMEM((NB, BLK, BLK), F32), pltpu.VMEM((NB * (NB + 1) // 2, BLK, BLK), F32),
                        pltpu.VMEM((NB, BLK, 2 * dh), BF16), pltpu.VMEM((NB, BLK, 2 * dh), BF16),
                        pltpu.VMEM((NB, BLK, 2 * dh), BF16)],
        compiler_params=_params("parallel", "arbitrary"), name="moba")(rel_bias, zB, zB, zB, qk_norm_l)


def kernel(x, norm_mix, w_in, gate_bias, conv_qk_w, conv_qk_b, mlstm_norm, qk_norm, rel_bias, w_out, norm_ffn,
           w_up, conv_ffn_w, conv_ffn_b, w_down):
    B, S, D = x.shape
    depth = w_in.shape[0]
    H = MLSTM_HEADS
    dm = mlstm_norm.shape[1] * mlstm_norm.shape[2]
    da = w_out.shape[1] - dm
    n_gate = 2 * H
    g0 = 4 * dm
    assert w_in.shape[2] == g0 + n_gate + 3 * da

    gb = jnp.pad(gate_bias, ((0, 0), (0, LANES - n_gate)))

    w_in_t = jnp.swapaxes(w_in, 1, 2)
    w_down_bf = w_down.astype(BF16)

    x2 = x.reshape(B * S, D)
    for l in range(depth):
        h, gates = _norm(x2, norm_mix[l], w_in_t, l, g0)
        zA = _proj_t(h, w_in_t, l, 0, g0)
        zB = _proj_t(h, w_in_t, l, g0 + n_gate, 3 * da)
        hm = _mlstm(zA.reshape(B, S, g0), gates.reshape(B, S, LANES), gb[l:l + 1], conv_qk_w[l], conv_qk_b[l],
                    mlstm_norm[l])
        ha = _moba(zB.reshape(B, S, 3 * da), qk_norm[l], rel_bias)
        x2 = _out_proj(hm.reshape(B * S, dm), ha.reshape(B * S, da), w_out, l, x2)
        h2 = _norm(x2, norm_ffn[l])
        a = _up_proj(h2, w_up, l, conv_ffn_w[l], conv_ffn_b[l], S)
        x2 = _down_proj(a, w_down_bf, l, x2)
    return x2.reshape(B, S, D)
```

```python
import functools
import math

import numpy as np
import jax
import jax.numpy as jnp
from jax import lax
from jax.experimental import pallas as pl
from jax.experimental.pallas import tpu as pltpu

F32 = jnp.float32
BF16 = jnp.bfloat16

EPS = 1e-6
MLSTM_HEADS = 4
MLSTM_CONV = 4
MLSTM_CHUNK = 256
MOBA_HEADS = 8
MOBA_BLOCK = 256
MOBA_TOPK = 3
MASKED = -1e30
N_BUCKETS = 32
MAX_DISTANCE = 2048
FFN_CONV = 3
HALO = 8
UP_ROWS = 256
UP_COLS = 256

LANES = 128
VMEM_LIMIT = 56 * 1024 * 1024


def _t5_thresholds():
    max_exact = N_BUCKETS // 2
    d = np.arange(0, MAX_DISTANCE, dtype=np.float64)
    ratio = np.maximum(d, max_exact) / max_exact
    large = max_exact + (np.log(ratio) / math.log(MAX_DISTANCE / max_exact)
                         * (N_BUCKETS - max_exact)).astype(np.int64)
    bucket = np.where(d < max_exact, d.astype(np.int64), np.minimum(large, N_BUCKETS - 1))
    assert np.all(np.diff(bucket) >= 0)
    thr = [int(np.argmax(bucket >= k)) if np.any(bucket >= k) else MAX_DISTANCE for k in range(N_BUCKETS)]
    return bucket, thr


_T5_BUCKET, _T5_THR = _t5_thresholds()


def _sigmoid(x):
    return 1.0 / (1.0 + jnp.exp(-x))


def _split3(x):
    hi = x.astype(BF16)
    r1 = x - hi.astype(F32)
    mid = r1.astype(BF16)
    lo = (r1 - mid.astype(F32)).astype(BF16)
    return hi, mid, lo


def _dot(a, b):
    return jnp.dot(a, b, preferred_element_type=F32)


def _dot_nt(a, b):
    return lax.dot_general(a, b, (((1,), (1,)), ((), ())), preferred_element_type=F32)


def _dot_tn(a, b):
    return lax.dot_general(a, b, (((0,), (0,)), ((), ())), preferred_element_type=F32)


def _params(*sem, flags=None):
    return pltpu.CompilerParams(dimension_semantics=sem, vmem_limit_bytes=VMEM_LIMIT, flags=flags)


def _rms(x, g):
    return x * lax.rsqrt(jnp.mean(x * x, axis=-1, keepdims=True) + EPS) * g


def _norm_kernel(x_ref, g_ref, h_ref):
    h_ref[...] = _rms(x_ref[...], g_ref[...]).astype(BF16)


def _norm_gates_kernel(x_ref, g_ref, wg_ref, h_ref, gates_ref):
    h = _rms(x_ref[...], g_ref[...]).astype(BF16)
    h_ref[...] = h
    gates_ref[...] = _dot_nt(h, wg_ref[...].astype(BF16))


def _norm(x2, g, wt=None, layer=0, gate_row=None, tm=1024):
    T, D = x2.shape
    x_spec = pl.BlockSpec((tm, D), lambda i: (i, 0))
    g_spec = pl.BlockSpec((1, D), lambda i: (0, 0))
    h_spec = pl.BlockSpec((tm, D), lambda i: (i, 0))
    if wt is None:
        return pl.pallas_call(
            _norm_kernel, grid=(T // tm,), in_specs=[x_spec, g_spec], out_specs=h_spec,
            out_shape=jax.ShapeDtypeStruct((T, D), BF16), compiler_params=_params("parallel"),
            name="rmsnorm")(x2, g.reshape(1, D))
    assert gate_row % LANES == 0
    return pl.pallas_call(
        _norm_gates_kernel, grid=(T // tm,),
        in_specs=[x_spec, g_spec, pl.BlockSpec((None, LANES, D), lambda i: (layer, gate_row // LANES, 0))],
        out_specs=[h_spec, pl.BlockSpec((tm, LANES), lambda i: (i, 0))],
        out_shape=[jax.ShapeDtypeStruct((T, D), BF16), jax.ShapeDtypeStruct((T, LANES), F32)],
        compiler_params=_params("parallel"), name="rmsnorm_gates")(x2, g.reshape(1, D), wt)


def _proj_t_kernel(h_ref, w_ref, o_ref, wbf_ref):
    @pl.when(pl.program_id(1) == 0)
    def _():
        wbf_ref[...] = w_ref[...].astype(BF16)

    o_ref[...] = _dot_nt(h_ref[...], wbf_ref[...]).astype(o_ref.dtype)


def _proj_t_shifted_kernel(h_ref, w_ref, wnext_ref, o_ref, wbf_ref, *, shift):
    @pl.when(pl.program_id(1) == 0)
    def _():
        wbf_ref[...] = jnp.concatenate([w_ref[shift:, :], wnext_ref[...]], axis=0).astype(BF16)

    o_ref[...] = _dot_nt(h_ref[...], wbf_ref[...]).astype(o_ref.dtype)


def _proj_t(h, wt, layer, row0, nrows, tm=1024, tn=1024, out_dtype=F32):
    T, K = h.shape
    shift = row0 % tn
    base = row0 - shift
    assert nrows % tn == 0 and T % tm == 0
    r0 = base // tn
    h_spec = pl.BlockSpec((tm, K), lambda n, m: (m, 0))
    w_spec = pl.BlockSpec((None, tn, K), lambda n, m: (layer, r0 + n, 0))
    common = dict(
        grid=(nrows // tn, T // tm),
        out_specs=pl.BlockSpec((tm, tn), lambda n, m: (m, n)),
        out_shape=jax.ShapeDtypeStruct((T, nrows), out_dtype),
        scratch_shapes=[pltpu.VMEM((tn, K), BF16)],
        compiler_params=_params("parallel", "arbitrary"), name="in_proj")
    if shift == 0:
        return pl.pallas_call(_proj_t_kernel, in_specs=[h_spec, w_spec], **common)(h, wt)
    assert shift % 16 == 0 or shift == 8, shift
    assert tn % shift == 0 and base % shift == 0
    per_tile = tn // shift
    wnext_spec = pl.BlockSpec((None, shift, K), lambda n, m: (layer, (r0 + n + 1) * per_tile, 0))
    return pl.pallas_call(functools.partial(_proj_t_shifted_kernel, shift=shift),
                          in_specs=[h_spec, w_spec, wnext_spec], **common)(h, wt, wt)


def _out_proj_kernel(hm_ref, ha_ref, w_ref, x_ref, o_ref, wbf_ref):
    @pl.when(pl.program_id(1) == 0)
    def _():
        wbf_ref[...] = w_ref[...].astype(BF16)

    km = hm_ref.shape[1]
    o_ref[...] = x_ref[...] + (_dot(hm_ref[...], wbf_ref[:km, :]) + _dot(ha_ref[...], wbf_ref[km:, :]))


def _out_proj(hm, ha, w, layer, x2, tm=1024, tn=1024):
    T, km = hm.shape
    ka = ha.shape[1]
    N = x2.shape[1]
    return pl.pallas_call(
        _out_proj_kernel, grid=(N // tn, T // tm),
        in_specs=[pl.BlockSpec((tm, km), lambda n, m: (m, 0)),
                  pl.BlockSpec((tm, ka), lambda n, m: (m, 0)),
                  pl.BlockSpec((None, km + ka, tn), lambda n, m: (layer, 0, n)),
                  pl.BlockSpec((tm, tn), lambda n, m: (m, n))],
        out_specs=pl.BlockSpec((tm, tn), lambda n, m: (m, n)),
        out_shape=jax.ShapeDtypeStruct((T, N), F32),
        scratch_shapes=[pltpu.VMEM((km + ka, tn), BF16)],
        compiler_params=_params("parallel", "arbitrary"), name="out_proj")(hm, ha, w, x2)


def _down_proj_kernel(a_ref, w_ref, x_ref, o_ref):
    o_ref[...] = x_ref[...] + _dot(a_ref[...], w_ref[...])


def _down_proj(a, w_bf, layer, x2, tm=512, tn=1024):
    T, K = a.shape
    N = x2.shape[1]
    return pl.pallas_call(
        _down_proj_kernel, grid=(N // tn, T // tm),
        in_specs=[pl.BlockSpec((tm, K), lambda n, m: (m, 0)),
                  pl.BlockSpec((None, K, tn), lambda n, m: (layer, 0, n)),
                  pl.BlockSpec((tm, tn), lambda n, m: (m, n))],
        out_specs=pl.BlockSpec((tm, tn), lambda n, m: (m, n)),
        out_shape=jax.ShapeDtypeStruct((T, N), F32),
        compiler_params=_params("parallel", "arbitrary"), name="down_proj")(a, w_bf, x2)


def _up_proj_kernel(h_ref, wg_ref, wv_ref, cwg_ref, cwv_ref, cbg_ref, cbv_ref, a_ref,
                    wgbf_ref, wvbf_ref, carry_g_ref, carry_v_ref, *, tiles_per_seq):
    m = pl.program_id(1)

    @pl.when(m == 0)
    def _():
        wgbf_ref[...] = wg_ref[...].astype(BF16)
        wvbf_ref[...] = wv_ref[...].astype(BF16)

    @pl.when(m % tiles_per_seq == 0)
    def _():
        carry_g_ref[...] = jnp.zeros_like(carry_g_ref)
        carry_v_ref[...] = jnp.zeros_like(carry_v_ref)

    tm, tf = a_ref.shape
    row8 = lax.broadcasted_iota(jnp.int32, (HALO, 1), 0)
    cwg, cwv, cbg, cbv = cwg_ref[...], cwv_ref[...], cbg_ref[...], cbv_ref[...]

    def shifted(u, tail, k):
        r = pltpu.roll(u, k, 0)
        top = jnp.where(row8 < k, pltpu.roll(tail, k, 0), r[:HALO])
        return jnp.concatenate([top, r[HALO:]], axis=0)

    def conv(u, tail, cw, cb):
        return ((cb + cw[0:1, :] * shifted(u, tail, 2)) + cw[1:2, :] * shifted(u, tail, 1)) + cw[2:3, :] * u

    def dot_cols(hb, w_ref):
        return jnp.concatenate([_dot(hb, w_ref[:, c0:c0 + UP_COLS]) for c0 in range(0, tf, UP_COLS)], axis=1)

    tail_g, tail_v = carry_g_ref[...], carry_v_ref[...]
    for r0 in range(0, tm, UP_ROWS):
        hb = h_ref[r0:r0 + UP_ROWS, :]
        ug = dot_cols(hb, wgbf_ref)
        uv = dot_cols(hb, wvbf_ref)
        yg = conv(ug, tail_g, cwg, cbg)
        yv = conv(uv, tail_v, cwv, cbv)
        a_ref[r0:r0 + UP_ROWS, :] = (yg * _sigmoid(yg) * yv).astype(BF16)
        tail_g, tail_v = ug[UP_ROWS - HALO:, :], uv[UP_ROWS - HALO:, :]
    carry_g_ref[...] = tail_g
    carry_v_ref[...] = tail_v


def _up_proj(h, w, layer, cw, cb, seq_len, tm=1024, tf=512):
    T, K = h.shape
    d_ff = w.shape[2] // 2
    nf = d_ff // tf
    assert d_ff % tf == 0 and seq_len % tm == 0
    kern = functools.partial(_up_proj_kernel, tiles_per_seq=seq_len // tm)
    return pl.pallas_call(
        kern, grid=(nf, T // tm),
        in_specs=[pl.BlockSpec((tm, K), lambda n, m: (m, 0)),
                  pl.BlockSpec((None, K, tf), lambda n, m: (layer, 0, n)),
                  pl.BlockSpec((None, K, tf), lambda n, m: (layer, 0, nf + n)),
                  pl.BlockSpec((FFN_CONV, tf), lambda n, m: (0, n)),
                  pl.BlockSpec((FFN_CONV, tf), lambda n, m: (0, nf + n)),
                  pl.BlockSpec((1, tf), lambda n, m: (0, n)),
                  pl.BlockSpec((1, tf), lambda n, m: (0, nf + n))],
        out_specs=pl.BlockSpec((tm, tf), lambda n, m: (m, n)),
        out_shape=jax.ShapeDtypeStruct((T, d_ff), BF16),
        scratch_shapes=[pltpu.VMEM((K, tf), BF16), pltpu.VMEM((K, tf), BF16),
                        pltpu.VMEM((HALO, tf), F32), pltpu.VMEM((HALO, tf), F32)],
        compiler_params=_params("parallel", "arbitrary"), name="up_proj")(
            h, w, w, cw, cw, cb.reshape(1, -1), cb.reshape(1, -1))


def _mlstm_kernel(q_ref, k_ref, v_ref, o_ref, g_ref, gb_ref, cwq_ref, cwk_ref, cbq_ref, cbk_ref, nw_ref,
                  out_ref, pad_ref, qs_ref, ks_ref, sv_ref, kv_ref, *, chunk):
    head = pl.program_id(1)
    S, dh = q_ref.shape[1], q_ref.shape[2]
    L = chunk
    nc = S // L
    halo = 8

    def conv_silu(x_ref, cw_ref, cb_ref, dst_ref, scale):
        pad_ref[0:halo, :] = jnp.zeros((halo, dh), F32)
        pad_ref[halo:halo + S, :] = x_ref[0]
        cw = cw_ref[...]
        cb = cb_ref[...]
        for c in range(nc):
            y = cb
            for j in range(MLSTM_CONV):
                off = halo + c * L - (MLSTM_CONV - 1) + j
                y = y + cw[j:j + 1, :] * pad_ref[off:off + L, :]
            dst_ref[c * L:(c + 1) * L, :] = (y * _sigmoid(y) * scale).astype(BF16)

    conv_silu(q_ref, cwq_ref, cbq_ref, qs_ref, dh ** -0.5)
    conv_silu(k_ref, cwk_ref, cbk_ref, ks_ref, 1.0)

    rr = lax.broadcasted_iota(jnp.int32, (L, L), 0)
    cc = lax.broadcasted_iota(jnp.int32, (L, L), 1)
    tri = rr >= cc
    eye = rr == cc
    tri_bf = jnp.where(tri, 1.0, 0.0).astype(BF16)
    lane = lax.broadcasted_iota(jnp.int32, (L, LANES), 1)

    def lane_pick(x, idx):
        return jnp.sum(jnp.where(lane == idx, x, 0.0), axis=1, keepdims=True)

    def local(c):
        sl = slice(c * L, (c + 1) * L)
        G = g_ref[0, sl, :] + gb_ref[...]
        li = lane_pick(G, head)
        LF = jnp.minimum(G, 0.0) - jnp.log1p(jnp.exp(-jnp.abs(G)))
        hi, mid, lo = _split3(LF)
        b = lane_pick(_dot(tri_bf, hi) + _dot(tri_bf, mid) + _dot(tri_bf, lo), MLSTM_HEADS + head)
        r_row = jnp.sum(jnp.where(eye, li - b, 0.0), axis=0, keepdims=True)
        Dm = jnp.where(tri, b + r_row, -jnp.inf)
        mrow = jnp.max(Dm, axis=1, keepdims=True)
        qc, kc, vc = qs_ref[sl, :], ks_ref[sl, :], v_ref[0, sl, :]
        s0 = _dot_nt(qc, kc) * jnp.exp(Dm - mrow)
        sv_ref[c] = _dot(s0.astype(BF16), vc.astype(BF16))
        bL = b[L - 1:L, :]
        gcol = bL - b + li
        gmax = jnp.max(gcol, axis=0, keepdims=True)
        wk0 = jnp.exp(gcol - gmax)
        kv_ref[c] = _dot_tn(kc, (wk0 * vc).astype(BF16))
        ksum = jnp.sum(wk0 * kc.astype(F32), axis=0, keepdims=True)
        return b, mrow, jnp.sum(s0, axis=1, keepdims=True), bL, gmax, ksum

    C = jnp.zeros((dh, dh), F32)
    n = jnp.zeros((1, dh), F32)
    m = jnp.zeros((1, 1), F32)
    for c in range(nc):
        sl = slice(c * L, (c + 1) * L)
        b, mrow, rs0, bL, gmax, ksum = local(c)
        inter = b + m
        m_t = jnp.maximum(inter, mrow)
        w_inter = jnp.exp(inter - m_t)
        f = jnp.exp(mrow - m_t)
        qc = qs_ref[sl, :]
        num = w_inter * _dot(qc, C.astype(BF16)) + f * sv_ref[c]
        den = w_inter * jnp.sum(qc.astype(F32) * n, axis=1, keepdims=True) + f * rs0
        hraw = num * (1.0 / jnp.maximum(jnp.abs(den), jnp.exp(-m_t)))
        y = _rms(hraw, nw_ref[...])
        out_ref[0, sl, :] = (y * _sigmoid(o_ref[0, sl, :])).astype(BF16)

        m_new = jnp.maximum(bL + m, gmax)
        a = jnp.exp(bL + m - m_new)
        fk = jnp.exp(gmax - m_new)
        C = a * C + fk * kv_ref[c]
        n = a * n + fk * ksum
        m = m_new


def _mlstm(zA, gates, gate_bias, conv_w, conv_b, norm_w, chunk=MLSTM_CHUNK):
    B, S, four_dm = zA.shape
    dm = four_dm // 4
    H = MLSTM_HEADS
    dh = dm // H
    kern = functools.partial(_mlstm_kernel, chunk=chunk)

    def col(base):
        return pl.BlockSpec((1, S, dh), lambda b, h: (b, 0, base + h))

    return pl.pallas_call(
        kern, grid=(B, H),
        in_specs=[col(0), col(H), col(2 * H), col(3 * H),
                  pl.BlockSpec((1, S, LANES), lambda b, h: (b, 0, 0)),
                  pl.BlockSpec((1, LANES), lambda b, h: (0, 0)),
                  pl.BlockSpec((MLSTM_CONV, dh), lambda b, h: (0, h)),
                  pl.BlockSpec((MLSTM_CONV, dh), lambda b, h: (0, H + h)),
                  pl.BlockSpec((1, dh), lambda b, h: (0, h)),
                  pl.BlockSpec((1, dh), lambda b, h: (0, H + h)),
                  pl.BlockSpec((None, 1, dh), lambda b, h: (h, 0, 0))],
        out_specs=pl.BlockSpec((1, S, dh), lambda b, h: (b, 0, h)),
        out_shape=jax.ShapeDtypeStruct((B, S, dm), BF16),
        scratch_shapes=[pltpu.VMEM((S + 8, dh), F32), pltpu.VMEM((S, dh), BF16), pltpu.VMEM((S, dh), BF16),
                        pltpu.VMEM((S // chunk, chunk, dh), F32), pltpu.VMEM((S // chunk, dh, dh), F32)],
        compiler_params=_params("parallel", "parallel"), name="mlstm")(
            zA, zA, zA, zA, gates, gate_bias, conv_w, conv_w, conv_b.reshape(1, -1), conv_b.reshape(1, -1),
            norm_w.reshape(H, 1, dh))


def _moba_kernel(rb_ref, q_ref, k_ref, v_ref, qkn_ref, out_ref,
                 bias_ref, sc_ref, qa_ref, ka_ref, va_ref):
    head = pl.program_id(0)
    S, dh = q_ref.shape[1], q_ref.shape[2]
    BLK = MOBA_BLOCK
    NB = S // BLK
    scale = dh ** -0.5

    @pl.when(pl.program_id(1) == 0)
    def _build_bias():
        rr = lax.broadcasted_iota(jnp.int32, (BLK, BLK), 0)
        cc = lax.broadcasted_iota(jnp.int32, (BLK, BLK), 1)
        for delta in range(NB):
            d = (rr - cc) + delta * BLK
            lo, hi = max(delta * BLK - (BLK - 1), 0), delta * BLK + BLK - 1
            tile = jnp.full((BLK, BLK), rb_ref[int(_T5_BUCKET[lo]), head], F32)
            for kb in range(int(_T5_BUCKET[lo]) + 1, int(_T5_BUCKET[hi]) + 1):
                tile = jnp.where(d >= _T5_THR[kb], rb_ref[kb, head], tile)
            if delta == 0:
                tile = jnp.where(d >= 0, tile, -jnp.inf)
            bias_ref[delta] = tile

    lane_k = lax.broadcasted_iota(jnp.int32, (BLK, dh), 1)
    blk = lax.broadcasted_iota(jnp.int32, (NB, BLK), 0)
    kmeans = []

    def prep(i):
        rows = slice(i * BLK, (i + 1) * BLK)
        qn = _rms(q_ref[0, rows, :], qkn_ref[0:1, :])
        kn = _rms(k_ref[0, rows, :], qkn_ref[1:2, :])
        q_hi = qn.astype(BF16)
        qa_ref[i, :, :dh] = q_hi
        ka_ref[i, :, :dh] = kn.astype(BF16)
        ka_ref[i, :, dh:] = jnp.where(lane_k == i, 1.0, 0.0).astype(BF16)
        va_ref[i, :, :dh] = v_ref[0, rows, :].astype(BF16)
        va_ref[i, :, dh:] = jnp.ones((BLK, dh), BF16)
        if i == 0:
            qa_ref[0, :, dh:] = jnp.zeros((BLK, dh), BF16)
        else:
            kmean = jnp.concatenate(kmeans + [jnp.zeros((16 - i, dh), F32)], axis=0)
            q_lo = (qn - q_hi.astype(F32)).astype(BF16)
            m_hi = kmean.astype(BF16)
            m_lo = (kmean - m_hi.astype(F32)).astype(BF16)
            g = (_dot_nt(m_hi, q_hi) + (_dot_nt(m_lo, q_hi) + _dot_nt(m_hi, q_lo)))[:NB]
            cnt = jnp.zeros((NB, BLK), F32)
            for jp in range(i):
                gj = g[jp:jp + 1, :]
                beats = (gj > g) | ((gj == g) & (blk > jp))
                cnt = cnt + jnp.where(beats, 1.0, 0.0)
            mask_t = jnp.where((cnt < MOBA_TOPK) | (blk >= i), 0.0, MASKED)
            mask_t = jnp.concatenate([mask_t, jnp.zeros((dh - NB, BLK), F32)], axis=0)
            qa_ref[i, :, dh:] = mask_t.T.astype(BF16)
        kmeans.append(jnp.mean(kn, axis=0, keepdims=True))

    tile = lambda i, j: i * (i + 1) // 2 + j

    def scores(i):
        qa = qa_ref[i]
        mx = None
        for j in range(i + 1):
            s = _dot_nt(qa, ka_ref[j]) * scale + bias_ref[i - j]
            sc_ref[tile(i, j)] = s
            mj = jnp.maximum(s[:, :LANES], s[:, LANES:])
            mx = mj if mx is None else jnp.maximum(mx, mj)
        return jnp.max(mx, axis=1, keepdims=True)

    def attend(i, mx):
        acc = jnp.zeros((BLK, 2 * dh), F32)
        for j in range(i + 1):
            acc = acc + _dot(jnp.exp(sc_ref[tile(i, j)] - mx).astype(BF16), va_ref[j])
        out_ref[0, i * BLK:(i + 1) * BLK, :] = (acc[:, :dh] * (1.0 / acc[:, dh:dh + 1])).astype(BF16)

    prep(0)
    mx_prev = None
    for i in range(NB):
        if i + 1 < NB:
            prep(i + 1)
        mx = scores(i)
        if i > 0:
            attend(i - 1, mx_prev)
        mx_prev = mx
    attend(NB - 1, mx_prev)


def _moba(zB, qk_norm_l, rel_bias):
    B, S, three_da = zB.shape
    da = three_da // 3
    H = MOBA_HEADS
    dh = da // H
    NB = S // MOBA_BLOCK
    BLK = MOBA_BLOCK

    def col(base):
        return pl.BlockSpec((1, S, dh), lambda h, b: (b, 0, base + h))

    return pl.pallas_call(
        _moba_kernel, grid=(H, B),
        in_specs=[pl.BlockSpec(memory_space=pltpu.SMEM),
                  col(0), col(H), col(2 * H),
                  pl.BlockSpec((2, dh), lambda h, b: (0, 0))],
        out_specs=pl.BlockSpec((1, S, dh), lambda h, b: (b, 0, h)),
        out_shape=jax.ShapeDtypeStruct((B, S, da), BF16),
        scratch_shapes=[pltpu.VMEM((NB, BLK, BLK), F32), pltpu.VMEM((NB * (NB + 1) // 2, BLK, BLK), F32),
                        pltpu.VMEM((NB, BLK, 2 * dh), BF16), pltpu.VMEM((NB, BLK, 2 * dh), BF16),
                        pltpu.VMEM((NB, BLK, 2 * dh), BF16)],
        compiler_params=_params("parallel", "arbitrary"), name="moba")(rel_bias, zB, zB, zB, qk_norm_l)


def kernel(x, norm_mix, w_in, gate_bias, conv_qk_w, conv_qk_b, mlstm_norm, qk_norm, rel_bias, w_out, norm_ffn,
           w_up, conv_ffn_w, conv_ffn_b, w_down):
    B, S, D = x.shape
    depth = w_in.shape[0]
    H = MLSTM_HEADS
    dm = mlstm_norm.shape[1] * mlstm_norm.shape[2]
    da = w_out.shape[1] - dm
    n_gate = 2 * H
    g0 = 4 * dm
    assert w_in.shape[2] == g0 + n_gate + 3 * da

    gb = jnp.pad(gate_bias, ((0, 0), (0, LANES - n_gate)))

    w_in_t = jnp.swapaxes(w_in, 1, 2)
    w_down_bf = w_down.astype(BF16)

    x2 = x.reshape(B * S, D)
    for l in range(depth):
        h, gates = _norm(x2, norm_mix[l], w_in_t, l, g0)
        zA = _proj_t(h, w_in_t, l, 0, g0)
        zB = _proj_t(h, w_in_t, l, g0 + n_gate, 3 * da)
        hm = _mlstm(zA.reshape(B, S, g0), gates.reshape(B, S, LANES), gb[l:l + 1], conv_qk_w[l], conv_qk_b[l],
                    mlstm_norm[l])
        ha = _moba(zB.reshape(B, S, 3 * da), qk_norm[l], rel_bias)
        x2 = _out_proj(hm.reshape(B * S, dm), ha.reshape(B * S, da), w_out, l, x2)
        h2 = _norm(x2, norm_ffn[l])
        a = _up_proj(h2, w_up, l, conv_ffn_w[l], conv_ffn_b[l], S)
        x2 = _down_proj(a, w_down_bf, l, x2)
    return x2.reshape(B, S, D)
```

```python
import functools
import math

import numpy as np
import jax
import jax.numpy as jnp
from jax import lax
from jax.experimental import pallas as pl
from jax.experimental.pallas import tpu as pltpu

F32 = jnp.float32
BF16 = jnp.bfloat16

EPS = 1e-6
MLSTM_HEADS = 4
MLSTM_CONV = 4
MLSTM_CHUNK = 256
MOBA_HEADS = 8
MOBA_BLOCK = 256
MOBA_TOPK = 3
MASKED = -1e30
N_BUCKETS = 32
MAX_DISTANCE = 2048
FFN_CONV = 3
HALO = 8
UP_ROWS = 128
UP_COLS = 256

LANES = 128
VMEM_LIMIT = 56 * 1024 * 1024


def _t5_thresholds():
    max_exact = N_BUCKETS // 2
    d = np.arange(0, MAX_DISTANCE, dtype=np.float64)
    ratio = np.maximum(d, max_exact) / max_exact
    large = max_exact + (np.log(ratio) / math.log(MAX_DISTANCE / max_exact)
                         * (N_BUCKETS - max_exact)).astype(np.int64)
    bucket = np.where(d < max_exact, d.astype(np.int64), np.minimum(large, N_BUCKETS - 1))
    assert np.all(np.diff(bucket) >= 0)
    thr = [int(np.argmax(bucket >= k)) if np.any(bucket >= k) else MAX_DISTANCE for k in range(N_BUCKETS)]
    return bucket, thr


_T5_BUCKET, _T5_THR = _t5_thresholds()


def _sigmoid(x):
    return 1.0 / (1.0 + jnp.exp(-x))


def _split3(x):
    hi = x.astype(BF16)
    r1 = x - hi.astype(F32)
    mid = r1.astype(BF16)
    lo = (r1 - mid.astype(F32)).astype(BF16)
    return hi, mid, lo


def _dot(a, b):
    return jnp.dot(a, b, preferred_element_type=F32)


def _dot_nt(a, b):
    return lax.dot_general(a, b, (((1,), (1,)), ((), ())), preferred_element_type=F32)


def _dot_tn(a, b):
    return lax.dot_general(a, b, (((0,), (0,)), ((), ())), preferred_element_type=F32)


def _params(*sem, flags=None):
    return pltpu.CompilerParams(dimension_semantics=sem, vmem_limit_bytes=VMEM_LIMIT, flags=flags)


def _rms(x, g):
    return x * lax.rsqrt(jnp.mean(x * x, axis=-1, keepdims=True) + EPS) * g


def _norm_kernel(x_ref, g_ref, h_ref):
    h_ref[...] = _rms(x_ref[...], g_ref[...]).astype(BF16)


def _norm_gates_kernel(x_ref, g_ref, wg_ref, h_ref, gates_ref):
    h = _rms(x_ref[...], g_ref[...]).astype(BF16)
    h_ref[...] = h
    gates_ref[...] = _dot_nt(h, wg_ref[...].astype(BF16))


def _norm(x2, g, wt=None, layer=0, gate_row=None, tm=1024):
    T, D = x2.shape
    x_spec = pl.BlockSpec((tm, D), lambda i: (i, 0))
    g_spec = pl.BlockSpec((1, D), lambda i: (0, 0))
    h_spec = pl.BlockSpec((tm, D), lambda i: (i, 0))
    if wt is None:
        return pl.pallas_call(
            _norm_kernel, grid=(T // tm,), in_specs=[x_spec, g_spec], out_specs=h_spec,
            out_shape=jax.ShapeDtypeStruct((T, D), BF16), compiler_params=_params("parallel"),
            name="rmsnorm")(x2, g.reshape(1, D))
    assert gate_row % LANES == 0
    return pl.pallas_call(
        _norm_gates_kernel, grid=(T // tm,),
        in_specs=[x_spec, g_spec, pl.BlockSpec((None, LANES, D), lambda i: (layer, gate_row // LANES, 0))],
        out_specs=[h_spec, pl.BlockSpec((tm, LANES), lambda i: (i, 0))],
        out_shape=[jax.ShapeDtypeStruct((T, D), BF16), jax.ShapeDtypeStruct((T, LANES), F32)],
        compiler_params=_params("parallel"), name="rmsnorm_gates")(x2, g.reshape(1, D), wt)


def _proj_t_kernel(h_ref, w_ref, o_ref, wbf_ref):
    @pl.when(pl.program_id(1) == 0)
    def _():
        wbf_ref[...] = w_ref[...].astype(BF16)

    o_ref[...] = _dot_nt(h_ref[...], wbf_ref[...]).astype(o_ref.dtype)


def _proj_t_shifted_kernel(h_ref, w_ref, wnext_ref, o_ref, wbf_ref, *, shift):
    @pl.when(pl.program_id(1) == 0)
    def _():
        wbf_ref[...] = jnp.concatenate([w_ref[shift:, :], wnext_ref[...]], axis=0).astype(BF16)

    o_ref[...] = _dot_nt(h_ref[...], wbf_ref[...]).astype(o_ref.dtype)


def _proj_t(h, wt, layer, row0, nrows, tm=1024, tn=1024, out_dtype=F32):
    T, K = h.shape
    shift = row0 % tn
    base = row0 - shift
    assert nrows % tn == 0 and T % tm == 0
    r0 = base // tn
    h_spec = pl.BlockSpec((tm, K), lambda n, m: (m, 0))
    w_spec = pl.BlockSpec((None, tn, K), lambda n, m: (layer, r0 + n, 0))
    common = dict(
        grid=(nrows // tn, T // tm),
        out_specs=pl.BlockSpec((tm, tn), lambda n, m: (m, n)),
        out_shape=jax.ShapeDtypeStruct((T, nrows), out_dtype),
        scratch_shapes=[pltpu.VMEM((tn, K), BF16)],
        compiler_params=_params("parallel", "arbitrary"), name="in_proj")
    if shift == 0:
        return pl.pallas_call(_proj_t_kernel, in_specs=[h_spec, w_spec], **common)(h, wt)
    assert shift % 16 == 0 or shift == 8, shift
    assert tn % shift == 0 and base % shift == 0
    per_tile = tn // shift
    wnext_spec = pl.BlockSpec((None, shift, K), lambda n, m: (layer, (r0 + n + 1) * per_tile, 0))
    return pl.pallas_call(functools.partial(_proj_t_shifted_kernel, shift=shift),
                          in_specs=[h_spec, w_spec, wnext_spec], **common)(h, wt, wt)


def _out_proj_kernel(hm_ref, ha_ref, w_ref, x_ref, o_ref, wbf_ref):
    @pl.when(pl.program_id(1) == 0)
    def _():
        wbf_ref[...] = w_ref[...].astype(BF16)

    km = hm_ref.shape[1]
    o_ref[...] = x_ref[...] + (_dot(hm_ref[...], wbf_ref[:km, :]) + _dot(ha_ref[...], wbf_ref[km:, :]))


def _out_proj(hm, ha, w, layer, x2, tm=1024, tn=1024):
    T, km = hm.shape
    ka = ha.shape[1]
    N = x2.shape[1]
    return pl.pallas_call(
        _out_proj_kernel, grid=(N // tn, T // tm),
        in_specs=[pl.BlockSpec((tm, km), lambda n, m: (m, 0)),
                  pl.BlockSpec((tm, ka), lambda n, m: (m, 0)),
                  pl.BlockSpec((None, km + ka, tn), lambda n, m: (layer, 0, n)),
                  pl.BlockSpec((tm, tn), lambda n, m: (m, n))],
        out_specs=pl.BlockSpec((tm, tn), lambda n, m: (m, n)),
        out_shape=jax.ShapeDtypeStruct((T, N), F32),
        scratch_shapes=[pltpu.VMEM((km + ka, tn), BF16)],
        compiler_params=_params("parallel", "arbitrary"), name="out_proj")(hm, ha, w, x2)


def _down_proj_kernel(a_ref, w_ref, x_ref, o_ref):
    o_ref[...] = x_ref[...] + _dot(a_ref[...], w_ref[...])


def _down_proj(a, w_bf, layer, x2, tm=512, tn=1024):
    T, K = a.shape
    N = x2.shape[1]
    return pl.pallas_call(
        _down_proj_kernel, grid=(N // tn, T // tm),
        in_specs=[pl.BlockSpec((tm, K), lambda n, m: (m, 0)),
                  pl.BlockSpec((None, K, tn), lambda n, m: (layer, 0, n)),
                  pl.BlockSpec((tm, tn), lambda n, m: (m, n))],
        out_specs=pl.BlockSpec((tm, tn), lambda n, m: (m, n)),
        out_shape=jax.ShapeDtypeStruct((T, N), F32),
        compiler_params=_params("parallel", "arbitrary"), name="down_proj")(a, w_bf, x2)


def _up_proj_kernel(h_ref, wg_ref, wv_ref, cwg_ref, cwv_ref, cbg_ref, cbv_ref, a_ref,
                    wgbf_ref, wvbf_ref, carry_g_ref, carry_v_ref, *, tiles_per_seq):
    m = pl.program_id(1)

    @pl.when(m == 0)
    def _():
        wgbf_ref[...] = wg_ref[...].astype(BF16)
        wvbf_ref[...] = wv_ref[...].astype(BF16)

    @pl.when(m % tiles_per_seq == 0)
    def _():
        carry_g_ref[...] = jnp.zeros_like(carry_g_ref)
        carry_v_ref[...] = jnp.zeros_like(carry_v_ref)

    tm, tf = a_ref.shape
    row8 = lax.broadcasted_iota(jnp.int32, (HALO, 1), 0)
    cwg, cwv, cbg, cbv = cwg_ref[...], cwv_ref[...], cbg_ref[...], cbv_ref[...]

    def shifted(u, tail, k):
        r = pltpu.roll(u, k, 0)
        top = jnp.where(row8 < k, pltpu.roll(tail, k, 0), r[:HALO])
        return jnp.concatenate([top, r[HALO:]], axis=0)

    def conv(u, tail, cw, cb):
        return ((cb + cw[0:1, :] * shifted(u, tail, 2)) + cw[1:2, :] * shifted(u, tail, 1)) + cw[2:3, :] * u

    def dot_cols(hb, w_ref):
        return jnp.concatenate([_dot(hb, w_ref[:, c0:c0 + UP_COLS]) for c0 in range(0, tf, UP_COLS)], axis=1)

    tail_g, tail_v = carry_g_ref[...], carry_v_ref[...]
    for r0 in range(0, tm, UP_ROWS):
        hb = h_ref[r0:r0 + UP_ROWS, :]
        ug = dot_cols(hb, wgbf_ref)
        uv = dot_cols(hb, wvbf_ref)
        yg = conv(ug, tail_g, cwg, cbg)
        yv = conv(uv, tail_v, cwv, cbv)
        a_ref[r0:r0 + UP_ROWS, :] = (yg * _sigmoid(yg) * yv).astype(BF16)
        tail_g, tail_v = ug[UP_ROWS - HALO:, :], uv[UP_ROWS - HALO:, :]
    carry_g_ref[...] = tail_g
    carry_v_ref[...] = tail_v


def _up_proj(h, w, layer, cw, cb, seq_len, tm=1024, tf=512):
    T, K = h.shape
    d_ff = w.shape[2] // 2
    nf = d_ff // tf
    assert d_ff % tf == 0 and seq_len % tm == 0
    kern = functools.partial(_up_proj_kernel, tiles_per_seq=seq_len // tm)
    return pl.pallas_call(
        kern, grid=(nf, T // tm),
        in_specs=[pl.BlockSpec((tm, K), lambda n, m: (m, 0)),
                  pl.BlockSpec((None, K, tf), lambda n, m: (layer, 0, n)),
                  pl.BlockSpec((None, K, tf), lambda n, m: (layer, 0, nf + n)),
                  pl.BlockSpec((FFN_CONV, tf), lambda n, m: (0, n)),
                  pl.BlockSpec((FFN_CONV, tf), lambda n, m: (0, nf + n)),
                  pl.BlockSpec((1, tf), lambda n, m: (0, n)),
                  pl.BlockSpec((1, tf), lambda n, m: (0, nf + n))],
        out_specs=pl.BlockSpec((tm, tf), lambda n, m: (m, n)),
        out_shape=jax.ShapeDtypeStruct((T, d_ff), BF16),
        scratch_shapes=[pltpu.VMEM((K, tf), BF16), pltpu.VMEM((K, tf), BF16),
                        pltpu.VMEM((HALO, tf), F32), pltpu.VMEM((HALO, tf), F32)],
        compiler_params=_params("parallel", "arbitrary"), name="up_proj")(
            h, w, w, cw, cw, cb.reshape(1, -1), cb.reshape(1, -1))


def _mlstm_kernel(q_ref, k_ref, v_ref, o_ref, g_ref, gb_ref, cwq_ref, cwk_ref, cbq_ref, cbk_ref, nw_ref,
                  out_ref, pad_ref, qs_ref, ks_ref, sv_ref, kv_ref, *, chunk):
    head = pl.program_id(1)
    S, dh = q_ref.shape[1], q_ref.shape[2]
    L = chunk
    nc = S // L
    halo = 8

    def conv_silu(x_ref, cw_ref, cb_ref, dst_ref, scale):
        pad_ref[0:halo, :] = jnp.zeros((halo, dh), F32)
        pad_ref[halo:halo + S, :] = x_ref[0]
        cw = cw_ref[...]
        cb = cb_ref[...]
        for c in range(nc):
            y = cb
            for j in range(MLSTM_CONV):
                off = halo + c * L - (MLSTM_CONV - 1) + j
                y = y + cw[j:j + 1, :] * pad_ref[off:off + L, :]
            dst_ref[c * L:(c + 1) * L, :] = (y * _sigmoid(y) * scale).astype(BF16)

    conv_silu(q_ref, cwq_ref, cbq_ref, qs_ref, dh ** -0.5)
    conv_silu(k_ref, cwk_ref, cbk_ref, ks_ref, 1.0)

    rr = lax.broadcasted_iota(jnp.int32, (L, L), 0)
    cc = lax.broadcasted_iota(jnp.int32, (L, L), 1)
    tri = rr >= cc
    eye = rr == cc
    tri_bf = jnp.where(tri, 1.0, 0.0).astype(BF16)
    lane = lax.broadcasted_iota(jnp.int32, (L, LANES), 1)

    def lane_pick(x, idx):
        return jnp.sum(jnp.where(lane == idx, x, 0.0), axis=1, keepdims=True)

    def local(c):
        sl = slice(c * L, (c + 1) * L)
        G = g_ref[0, sl, :] + gb_ref[...]
        li = lane_pick(G, head)
        LF = jnp.minimum(G, 0.0) - jnp.log1p(jnp.exp(-jnp.abs(G)))
        hi, mid, lo = _split3(LF)
        b = lane_pick(_dot(tri_bf, hi) + _dot(tri_bf, mid) + _dot(tri_bf, lo), MLSTM_HEADS + head)
        r_row = jnp.sum(jnp.where(eye, li - b, 0.0), axis=0, keepdims=True)
        Dm = jnp.where(tri, b + r_row, -jnp.inf)
        mrow = jnp.max(Dm, axis=1, keepdims=True)
        qc, kc, vc = qs_ref[sl, :], ks_ref[sl, :], v_ref[0, sl, :]
        s0 = _dot_nt(qc, kc) * jnp.exp(Dm - mrow)
        sv_ref[c] = _dot(s0.astype(BF16), vc.astype(BF16))
        bL = b[L - 1:L, :]
        gcol = bL - b + li
        gmax = jnp.max(gcol, axis=0, keepdims=True)
        wk0 = jnp.exp(gcol - gmax)
        kv_ref[c] = _dot_tn(kc, (wk0 * vc).astype(BF16))
        ksum = jnp.sum(wk0 * kc.astype(F32), axis=0, keepdims=True)
        return b, mrow, jnp.sum(s0, axis=1, keepdims=True), bL, gmax, ksum

    C = jnp.zeros((dh, dh), F32)
    n = jnp.zeros((1, dh), F32)
    m = jnp.zeros((1, 1), F32)
    for c in range(nc):
        sl = slice(c * L, (c + 1) * L)
        b, mrow, rs0, bL, gmax, ksum = local(c)
        inter = b + m
        m_t = jnp.maximum(inter, mrow)
        w_inter = jnp.exp(inter - m_t)
        f = jnp.exp(mrow - m_t)
        qc = qs_ref[sl, :]
        num = w_inter * _dot(qc, C.astype(BF16)) + f * sv_ref[c]
        den = w_inter * jnp.sum(qc.astype(F32) * n, axis=1, keepdims=True) + f * rs0
        hraw = num * (1.0 / jnp.maximum(jnp.abs(den), jnp.exp(-m_t)))
        y = _rms(hraw, nw_ref[...])
        out_ref[0, sl, :] = (y * _sigmoid(o_ref[0, sl, :])).astype(BF16)

        m_new = jnp.maximum(bL + m, gmax)
        a = jnp.exp(bL + m - m_new)
        fk = jnp.exp(gmax - m_new)
        C = a * C + fk * kv_ref[c]
        n = a * n + fk * ksum
        m = m_new


def _mlstm(zA, gates, gate_bias, conv_w, conv_b, norm_w, chunk=MLSTM_CHUNK):
    B, S, four_dm = zA.shape
    dm = four_dm // 4
    H = MLSTM_HEADS
    dh = dm // H
    kern = functools.partial(_mlstm_kernel, chunk=chunk)

    def col(base):
        return pl.BlockSpec((1, S, dh), lambda b, h: (b, 0, base + h))

    return pl.pallas_call(
        kern, grid=(B, H),
        in_specs=[col(0), col(H), col(2 * H), col(3 * H),
                  pl.BlockSpec((1, S, LANES), lambda b, h: (b, 0, 0)),
                  pl.BlockSpec((1, LANES), lambda b, h: (0, 0)),
                  pl.BlockSpec((MLSTM_CONV, dh), lambda b, h: (0, h)),
                  pl.BlockSpec((MLSTM_CONV, dh), lambda b, h: (0, H + h)),
                  pl.BlockSpec((1, dh), lambda b, h: (0, h)),
                  pl.BlockSpec((1, dh), lambda b, h: (0, H + h)),
                  pl.BlockSpec((None, 1, dh), lambda b, h: (h, 0, 0))],
        out_specs=pl.BlockSpec((1, S, dh), lambda b, h: (b, 0, h)),
        out_shape=jax.ShapeDtypeStruct((B, S, dm), BF16),
        scratch_shapes=[pltpu.VMEM((S + 8, dh), F32), pltpu.VMEM((S, dh), BF16), pltpu.VMEM((S, dh), BF16),
                        pltpu.VMEM((S // chunk, chunk, dh), F32), pltpu.VMEM((S // chunk, dh, dh), F32)],
        compiler_params=_params("parallel", "parallel"), name="mlstm")(
            zA, zA, zA, zA, gates, gate_bias, conv_w, conv_w, conv_b.reshape(1, -1), conv_b.reshape(1, -1),
            norm_w.reshape(H, 1, dh))


def _moba_kernel(rb_ref, q_ref, k_ref, v_ref, qkn_ref, out_ref,
                 bias_ref, sc_ref, qa_ref, ka_ref, va_ref):
    head = pl.program_id(0)
    S, dh = q_ref.shape[1], q_ref.shape[2]
    BLK = MOBA_BLOCK
    NB = S // BLK
    scale = dh ** -0.5

    @pl.when(pl.program_id(1) == 0)
    def _build_bias():
        rr = lax.broadcasted_iota(jnp.int32, (BLK, BLK), 0)
        cc = lax.broadcasted_iota(jnp.int32, (BLK, BLK), 1)
        for delta in range(NB):
            d = (rr - cc) + delta * BLK
            lo, hi = max(delta * BLK - (BLK - 1), 0), delta * BLK + BLK - 1
            tile = jnp.full((BLK, BLK), rb_ref[int(_T5_BUCKET[lo]), head], F32)
            for kb in range(int(_T5_BUCKET[lo]) + 1, int(_T5_BUCKET[hi]) + 1):
                tile = jnp.where(d >= _T5_THR[kb], rb_ref[kb, head], tile)
            if delta == 0:
                tile = jnp.where(d >= 0, tile, -jnp.inf)
            bias_ref[delta] = tile

    lane_k = lax.broadcasted_iota(jnp.int32, (BLK, dh), 1)
    blk = lax.broadcasted_iota(jnp.int32, (NB, BLK), 0)
    kmeans = []

    def prep(i):
        rows = slice(i * BLK, (i + 1) * BLK)
        qn = _rms(q_ref[0, rows, :], qkn_ref[0:1, :])
        kn = _rms(k_ref[0, rows, :], qkn_ref[1:2, :])
        q_hi = qn.astype(BF16)
        qa_ref[i, :, :dh] = q_hi
        ka_ref[i, :, :dh] = kn.astype(BF16)
        ka_ref[i, :, dh:] = jnp.where(lane_k == i, 1.0, 0.0).astype(BF16)
        va_ref[i, :, :dh] = v_ref[0, rows, :].astype(BF16)
        va_ref[i, :, dh:] = jnp.ones((BLK, dh), BF16)
        if i == 0:
            qa_ref[0, :, dh:] = jnp.zeros((BLK, dh), BF16)
        else:
            kmean = jnp.concatenate(kmeans + [jnp.zeros((16 - i, dh), F32)], axis=0)
            q_lo = (qn - q_hi.astype(F32)).astype(BF16)
            m_hi = kmean.astype(BF16)
            m_lo = (kmean - m_hi.astype(F32)).astype(BF16)
            g = (_dot_nt(m_hi, q_hi) + (_dot_nt(m_lo, q_hi) + _dot_nt(m_hi, q_lo)))[:NB]
            cnt = jnp.zeros((NB, BLK), F32)
            for jp in range(i):
                gj = g[jp:jp + 1, :]
                beats = (gj > g) | ((gj == g) & (blk > jp))
                cnt = cnt + jnp.where(beats, 1.0, 0.0)
            mask_t = jnp.where((cnt < MOBA_TOPK) | (blk >= i), 0.0, MASKED)
            mask_t = jnp.concatenate([mask_t, jnp.zeros((dh - NB, BLK), F32)], axis=0)
            qa_ref[i, :, dh:] = mask_t.T.astype(BF16)
        kmeans.append(jnp.mean(kn, axis=0, keepdims=True))

    tile = lambda i, j: i * (i + 1) // 2 + j

    def scores(i):
        qa = qa_ref[i]
        mx = None
        for j in range(i + 1):
            s = _dot_nt(qa, ka_ref[j]) * scale + bias_ref[i - j]
            sc_ref[tile(i, j)] = s
            mj = jnp.maximum(s[:, :LANES], s[:, LANES:])
            mx = mj if mx is None else jnp.maximum(mx, mj)
        return jnp.max(mx, axis=1, keepdims=True)

    def attend(i, mx):
        acc = jnp.zeros((BLK, 2 * dh), F32)
        for j in range(i + 1):
            acc = acc + _dot(jnp.exp(sc_ref[tile(i, j)] - mx).astype(BF16), va_ref[j])
        out_ref[0, i * BLK:(i + 1) * BLK, :] = (acc[:, :dh] * (1.0 / acc[:, dh:dh + 1])).astype(BF16)

    prep(0)
    mx_prev = None
    for i in range(NB):
        if i + 1 < NB:
            prep(i + 1)
        mx = scores(i)
        if i > 0:
            attend(i - 1, mx_prev)
        mx_prev = mx
    attend(NB - 1, mx_prev)


def _moba(zB, qk_norm_l, rel_bias):
    B, S, three_da = zB.shape
    da = three_da // 3
    H = MOBA_HEADS
    dh = da // H
    NB = S // MOBA_BLOCK
    BLK = MOBA_BLOCK

    def col(base):
        return pl.BlockSpec((1, S, dh), lambda h, b: (b, 0, base + h))

    return pl.pallas_call(
        _moba_kernel, grid=(H, B),
        in_specs=[pl.BlockSpec(memory_space=pltpu.SMEM),
                  col(0), col(H), col(2 * H),
                  pl.BlockSpec((2, dh), lambda h, b: (0, 0))],
        out_specs=pl.BlockSpec((1, S, dh), lambda h, b: (b, 0, h)),
        out_shape=jax.ShapeDtypeStruct((B, S, da), BF16),
        scratch_shapes=[pltpu.VMEM((NB, BLK, BLK), F32), pltpu.VMEM((NB * (NB + 1) // 2, BLK, BLK), F32),
                        pltpu.VMEM((NB, BLK, 2 * dh), BF16), pltpu.VMEM((NB, BLK, 2 * dh), BF16),
                        pltpu.VMEM((NB, BLK, 2 * dh), BF16)],
        compiler_params=_params("parallel", "arbitrary"), name="moba")(rel_bias, zB, zB, zB, qk_norm_l)


def kernel(x, norm_mix, w_in, gate_bias, conv_qk_w, conv_qk_b, mlstm_norm, qk_norm, rel_bias, w_out, norm_ffn,
           w_up, conv_ffn_w, conv_ffn_b, w_down):
    B, S, D = x.shape
    depth = w_in.shape[0]
    H = MLSTM_HEADS
    dm = mlstm_norm.shape[1] * mlstm_norm.shape[2]
    da = w_out.shape[1] - dm
    n_gate = 2 * H
    g0 = 4 * dm
    assert w_in.shape[2] == g0 + n_gate + 3 * da

    gb = jnp.pad(gate_bias, ((0, 0), (0, LANES - n_gate)))

    w_in_t = jnp.swapaxes(w_in, 1, 2)
    w_down_bf = w_down.astype(BF16)

    x2 = x.reshape(B * S, D)
    for l in range(depth):
        h, gates = _norm(x2, norm_mix[l], w_in_t, l, g0)
        zA = _proj_t(h, w_in_t, l, 0, g0)
        zB = _proj_t(h, w_in_t, l, g0 + n_gate, 3 * da)
        hm = _mlstm(zA.reshape(B, S, g0), gates.reshape(B, S, LANES), gb[l:l + 1], conv_qk_w[l], conv_qk_b[l],
                    mlstm_norm[l])
        ha = _moba(zB.reshape(B, S, 3 * da), qk_norm[l], rel_bias)
        x2 = _out_proj(hm.reshape(B * S, dm), ha.reshape(B * S, da), w_out, l, x2)
        h2 = _norm(x2, norm_ffn[l])
        a = _up_proj(h2, w_up, l, conv_ffn_w[l], conv_ffn_b[l], S)
        x2 = _down_proj(a, w_down_bf, l, x2)
    return x2.reshape(B, S, D)
```

```python
import functools
import math

import numpy as np
import jax
import jax.numpy as jnp
from jax import lax
from jax.experimental import pallas as pl
from jax.experimental.pallas import tpu as pltpu

F32 = jnp.float32
BF16 = jnp.bfloat16

EPS = 1e-6
MLSTM_HEADS = 4
MLSTM_CONV = 4
MLSTM_CHUNK = 256
MOBA_HEADS = 8
MOBA_BLOCK = 256
MOBA_TOPK = 3
MASKED = -1e30
N_BUCKETS = 32
MAX_DISTANCE = 2048
FFN_CONV = 3
HALO = 8
UP_ROWS = 256
UP_COLS = 256

LANES = 128
VMEM_LIMIT = 56 * 1024 * 1024


def _t5_thresholds():
    max_exact = N_BUCKETS // 2
    d = np.arange(0, MAX_DISTANCE, dtype=np.float64)
    ratio = np.maximum(d, max_exact) / max_exact
    large = max_exact + (np.log(ratio) / math.log(MAX_DISTANCE / max_exact)
                         * (N_BUCKETS - max_exact)).astype(np.int64)
    bucket = np.where(d < max_exact, d.astype(np.int64), np.minimum(large, N_BUCKETS - 1))
    assert np.all(np.diff(bucket) >= 0)
    thr = [int(np.argmax(bucket >= k)) if np.any(bucket >= k) else MAX_DISTANCE for k in range(N_BUCKETS)]
    return bucket, thr


_T5_BUCKET, _T5_THR = _t5_thresholds()


def _sigmoid(x):
    return 1.0 / (1.0 + jnp.exp(-x))


def _split3(x):
    hi = x.astype(BF16)
    r1 = x - hi.astype(F32)
    mid = r1.astype(BF16)
    lo = (r1 - mid.astype(F32)).astype(BF16)
    return hi, mid, lo


def _dot(a, b):
    return jnp.dot(a, b, preferred_element_type=F32)


def _dot_nt(a, b):
    return lax.dot_general(a, b, (((1,), (1,)), ((), ())), preferred_element_type=F32)


def _dot_tn(a, b):
    return lax.dot_general(a, b, (((0,), (0,)), ((), ())), preferred_element_type=F32)


def _params(*sem, flags=None):
    return pltpu.CompilerParams(dimension_semantics=sem, vmem_limit_bytes=VMEM_LIMIT, flags=flags)


def _rms(x, g):
    return x * lax.rsqrt(jnp.mean(x * x, axis=-1, keepdims=True) + EPS) * g


def _norm_kernel(x_ref, g_ref, h_ref):
    h_ref[...] = _rms(x_ref[...], g_ref[...]).astype(BF16)


def _norm_gates_kernel(x_ref, g_ref, wg_ref, h_ref, gates_ref):
    h = _rms(x_ref[...], g_ref[...]).astype(BF16)
    h_ref[...] = h
    gates_ref[...] = _dot_nt(h, wg_ref[...].astype(BF16))


def _norm(x2, g, wt=None, layer=0, gate_row=None, tm=1024):
    T, D = x2.shape
    x_spec = pl.BlockSpec((tm, D), lambda i: (i, 0))
    g_spec = pl.BlockSpec((1, D), lambda i: (0, 0))
    h_spec = pl.BlockSpec((tm, D), lambda i: (i, 0))
    if wt is None:
        return pl.pallas_call(
            _norm_kernel, grid=(T // tm,), in_specs=[x_spec, g_spec], out_specs=h_spec,
            out_shape=jax.ShapeDtypeStruct((T, D), BF16), compiler_params=_params("parallel"),
            name="rmsnorm")(x2, g.reshape(1, D))
    assert gate_row % LANES == 0
    return pl.pallas_call(
        _norm_gates_kernel, grid=(T // tm,),
        in_specs=[x_spec, g_spec, pl.BlockSpec((None, LANES, D), lambda i: (layer, gate_row // LANES, 0))],
        out_specs=[h_spec, pl.BlockSpec((tm, LANES), lambda i: (i, 0))],
        out_shape=[jax.ShapeDtypeStruct((T, D), BF16), jax.ShapeDtypeStruct((T, LANES), F32)],
        compiler_params=_params("parallel"), name="rmsnorm_gates")(x2, g.reshape(1, D), wt)


def _proj_t_kernel(h_ref, w_ref, o_ref, wbf_ref):
    @pl.when(pl.program_id(1) == 0)
    def _():
        wbf_ref[...] = w_ref[...].astype(BF16)

    o_ref[...] = _dot_nt(h_ref[...], wbf_ref[...]).astype(o_ref.dtype)


def _proj_t_shifted_kernel(h_ref, w_ref, wnext_ref, o_ref, wbf_ref, *, shift):
    @pl.when(pl.program_id(1) == 0)
    def _():
        wbf_ref[...] = jnp.concatenate([w_ref[shift:, :], wnext_ref[...]], axis=0).astype(BF16)

    o_ref[...] = _dot_nt(h_ref[...], wbf_ref[...]).astype(o_ref.dtype)


def _proj_t(h, wt, layer, row0, nrows, tm=1024, tn=1024, out_dtype=F32):
    T, K = h.shape
    shift = row0 % tn
    base = row0 - shift
    assert nrows % tn == 0 and T % tm == 0
    r0 = base // tn
    h_spec = pl.BlockSpec((tm, K), lambda n, m: (m, 0))
    w_spec = pl.BlockSpec((None, tn, K), lambda n, m: (layer, r0 + n, 0))
    common = dict(
        grid=(nrows // tn, T // tm),
        out_specs=pl.BlockSpec((tm, tn), lambda n, m: (m, n)),
        out_shape=jax.ShapeDtypeStruct((T, nrows), out_dtype),
        scratch_shapes=[pltpu.VMEM((tn, K), BF16)],
        compiler_params=_params("parallel", "arbitrary"), name="in_proj")
    if shift == 0:
        return pl.pallas_call(_proj_t_kernel, in_specs=[h_spec, w_spec], **common)(h, wt)
    assert shift % 16 == 0 or shift == 8, shift
    assert tn % shift == 0 and base % shift == 0
    per_tile = tn // shift
    wnext_spec = pl.BlockSpec((None, shift, K), lambda n, m: (layer, (r0 + n + 1) * per_tile, 0))
    return pl.pallas_call(functools.partial(_proj_t_shifted_kernel, shift=shift),
                          in_specs=[h_spec, w_spec, wnext_spec], **common)(h, wt, wt)


def _out_proj_kernel(hm_ref, ha_ref, w_ref, x_ref, o_ref, wbf_ref):
    @pl.when(pl.program_id(1) == 0)
    def _():
        wbf_ref[...] = w_ref[...].astype(BF16)

    km = hm_ref.shape[1]
    o_ref[...] = x_ref[...] + (_dot(hm_ref[...], wbf_ref[:km, :]) + _dot(ha_ref[...], wbf_ref[km:, :]))


def _out_proj(hm, ha, w, layer, x2, tm=1024, tn=1024):
    T, km = hm.shape
    ka = ha.shape[1]
    N = x2.shape[1]
    return pl.pallas_call(
        _out_proj_kernel, grid=(N // tn, T // tm),
        in_specs=[pl.BlockSpec((tm, km), lambda n, m: (m, 0)),
                  pl.BlockSpec((tm, ka), lambda n, m: (m, 0)),
                  pl.BlockSpec((None, km + ka, tn), lambda n, m: (layer, 0, n)),
                  pl.BlockSpec((tm, tn), lambda n, m: (m, n))],
        out_specs=pl.BlockSpec((tm, tn), lambda n, m: (m, n)),
        out_shape=jax.ShapeDtypeStruct((T, N), F32),
        scratch_shapes=[pltpu.VMEM((km + ka, tn), BF16)],
        compiler_params=_params("parallel", "arbitrary"), name="out_proj")(hm, ha, w, x2)


def _down_proj_kernel(a_ref, w_ref, x_ref, o_ref):
    o_ref[...] = x_ref[...] + _dot(a_ref[...], w_ref[...])


def _down_proj(a, w_bf, x2, tm=512, tn=1024):
    T, K = a.shape
    N = x2.shape[1]
    return pl.pallas_call(
        _down_proj_kernel, grid=(N // tn, T // tm),
        in_specs=[pl.BlockSpec((tm, K), lambda n, m: (m, 0)),
                  pl.BlockSpec((K, tn), lambda n, m: (0, n)),
                  pl.BlockSpec((tm, tn), lambda n, m: (m, n))],
        out_specs=pl.BlockSpec((tm, tn), lambda n, m: (m, n)),
        out_shape=jax.ShapeDtypeStruct((T, N), F32),
        compiler_params=_params("parallel", "arbitrary"), name="down_proj")(a, w_bf, x2)


def _up_proj_kernel(h_ref, wg_ref, wv_ref, cwg_ref, cwv_ref, cbg_ref, cbv_ref, wd_ref, a_ref, wdbf_ref,
                    wgbf_ref, wvbf_ref, carry_g_ref, carry_v_ref, *, tiles_per_seq):
    m = pl.program_id(1)
    wdbf_ref[...] = wd_ref[...].astype(BF16)

    @pl.when(m == 0)
    def _():
        wgbf_ref[...] = wg_ref[...].astype(BF16)
        wvbf_ref[...] = wv_ref[...].astype(BF16)

    @pl.when(m % tiles_per_seq == 0)
    def _():
        carry_g_ref[...] = jnp.zeros_like(carry_g_ref)
        carry_v_ref[...] = jnp.zeros_like(carry_v_ref)

    tm, tf = a_ref.shape
    row8 = lax.broadcasted_iota(jnp.int32, (HALO, 1), 0)
    cwg, cwv, cbg, cbv = cwg_ref[...], cwv_ref[...], cbg_ref[...], cbv_ref[...]

    def shifted(u, tail, k):
        r = pltpu.roll(u, k, 0)
        top = jnp.where(row8 < k, pltpu.roll(tail, k, 0), r[:HALO])
        return jnp.concatenate([top, r[HALO:]], axis=0)

    def conv(u, tail, cw, cb):
        return ((cb + cw[0:1, :] * shifted(u, tail, 2)) + cw[1:2, :] * shifted(u, tail, 1)) + cw[2:3, :] * u

    def dot_cols(hb, w_ref):
        return jnp.concatenate([_dot(hb, w_ref[:, c0:c0 + UP_COLS]) for c0 in range(0, tf, UP_COLS)], axis=1)

    tail_g, tail_v = carry_g_ref[...], carry_v_ref[...]
    for r0 in range(0, tm, UP_ROWS):
        hb = h_ref[r0:r0 + UP_ROWS, :]
        ug = dot_cols(hb, wgbf_ref)
        uv = dot_cols(hb, wvbf_ref)
        yg = conv(ug, tail_g, cwg, cbg)
        yv = conv(uv, tail_v, cwv, cbv)
        a_ref[r0:r0 + UP_ROWS, :] = (yg * _sigmoid(yg) * yv).astype(BF16)
        tail_g, tail_v = ug[UP_ROWS - HALO:, :], uv[UP_ROWS - HALO:, :]
    carry_g_ref[...] = tail_g
    carry_v_ref[...] = tail_v


def _up_proj(h, w, layer, cw, cb, w_down, seq_len, tm=1024, tf=512):
    T, K = h.shape
    d_ff = w.shape[2] // 2
    nf = d_ff // tf
    nm = T // tm
    assert d_ff % tf == 0 and seq_len % tm == 0
    d_out = w_down.shape[2]
    slab = d_ff // (nf * nm)
    assert slab * nf * nm == d_ff and slab % 16 == 0
    kern = functools.partial(_up_proj_kernel, tiles_per_seq=seq_len // tm)
    return pl.pallas_call(
        kern, grid=(nf, nm),
        in_specs=[pl.BlockSpec((tm, K), lambda n, m: (m, 0)),
                  pl.BlockSpec((None, K, tf), lambda n, m: (layer, 0, n)),
                  pl.BlockSpec((None, K, tf), lambda n, m: (layer, 0, nf + n)),
                  pl.BlockSpec((FFN_CONV, tf), lambda n, m: (0, n)),
                  pl.BlockSpec((FFN_CONV, tf), lambda n, m: (0, nf + n)),
                  pl.BlockSpec((1, tf), lambda n, m: (0, n)),
                  pl.BlockSpec((1, tf), lambda n, m: (0, nf + n)),
                  pl.BlockSpec((None, slab, d_out), lambda n, m: (layer, n * nm + m, 0))],
        out_specs=[pl.BlockSpec((tm, tf), lambda n, m: (m, n)),
                   pl.BlockSpec((slab, d_out), lambda n, m: (n * nm + m, 0))],
        out_shape=[jax.ShapeDtypeStruct((T, d_ff), BF16), jax.ShapeDtypeStruct((d_ff, d_out), BF16)],
        scratch_shapes=[pltpu.VMEM((K, tf), BF16), pltpu.VMEM((K, tf), BF16),
                        pltpu.VMEM((HALO, tf), F32), pltpu.VMEM((HALO, tf), F32)],
        compiler_params=_params("parallel", "arbitrary"), name="up_proj")(
            h, w, w, cw, cw, cb.reshape(1, -1), cb.reshape(1, -1), w_down)


def _mlstm_kernel(q_ref, k_ref, v_ref, o_ref, g_ref, gb_ref, cwq_ref, cwk_ref, cbq_ref, cbk_ref, nw_ref,
                  out_ref, pad_ref, qs_ref, ks_ref, sv_ref, kv_ref, *, chunk):
    head = pl.program_id(1)
    S, dh = q_ref.shape[1], q_ref.shape[2]
    L = chunk
    nc = S // L
    halo = 8

    def conv_silu(x_ref, cw_ref, cb_ref, dst_ref, scale):
        pad_ref[0:halo, :] = jnp.zeros((halo, dh), F32)
        pad_ref[halo:halo + S, :] = x_ref[0]
        cw = cw_ref[...]
        cb = cb_ref[...]
        for c in range(nc):
            y = cb
            for j in range(MLSTM_CONV):
                off = halo + c * L - (MLSTM_CONV - 1) + j
                y = y + cw[j:j + 1, :] * pad_ref[off:off + L, :]
            dst_ref[c * L:(c + 1) * L, :] = (y * _sigmoid(y) * scale).astype(BF16)

    conv_silu(q_ref, cwq_ref, cbq_ref, qs_ref, dh ** -0.5)
    conv_silu(k_ref, cwk_ref, cbk_ref, ks_ref, 1.0)

    rr = lax.broadcasted_iota(jnp.int32, (L, L), 0)
    cc = lax.broadcasted_iota(jnp.int32, (L, L), 1)
    tri = rr >= cc
    eye = rr == cc
    tri_bf = jnp.where(tri, 1.0, 0.0).astype(BF16)
    lane = lax.broadcasted_iota(jnp.int32, (L, LANES), 1)

    def lane_pick(x, idx):
        return jnp.sum(jnp.where(lane == idx, x, 0.0), axis=1, keepdims=True)

    def local(c):
        sl = slice(c * L, (c + 1) * L)
        G = g_ref[0, sl, :] + gb_ref[...]
        li = lane_pick(G, head)
        LF = jnp.minimum(G, 0.0) - jnp.log1p(jnp.exp(-jnp.abs(G)))
        hi, mid, lo = _split3(LF)
        b = lane_pick(_dot(tri_bf, hi) + _dot(tri_bf, mid) + _dot(tri_bf, lo), MLSTM_HEADS + head)
        r_row = jnp.sum(jnp.where(eye, li - b, 0.0), axis=0, keepdims=True)
        Dm = jnp.where(tri, b + r_row, -jnp.inf)
        mrow = jnp.max(Dm, axis=1, keepdims=True)
        qc, kc, vc = qs_ref[sl, :], ks_ref[sl, :], v_ref[0, sl, :]
        s0 = _dot_nt(qc, kc) * jnp.exp(Dm - mrow)
        sv_ref[c] = _dot(s0.astype(BF16), vc.astype(BF16))
        bL = b[L - 1:L, :]
        gcol = bL - b + li
        gmax = jnp.max(gcol, axis=0, keepdims=True)
        wk0 = jnp.exp(gcol - gmax)
        kv_ref[c] = _dot_tn(kc, (wk0 * vc).astype(BF16))
        ksum = jnp.sum(wk0 * kc.astype(F32), axis=0, keepdims=True)
        return b, mrow, jnp.sum(s0, axis=1, keepdims=True), bL, gmax, ksum

    C = jnp.zeros((dh, dh), F32)
    n = jnp.zeros((1, dh), F32)
    m = jnp.zeros((1, 1), F32)
    for c in range(nc):
        sl = slice(c * L, (c + 1) * L)
        b, mrow, rs0, bL, gmax, ksum = local(c)
        inter = b + m
        m_t = jnp.maximum(inter, mrow)
        w_inter = jnp.exp(inter - m_t)
        f = jnp.exp(mrow - m_t)
        qc = qs_ref[sl, :]
        num = w_inter * _dot(qc, C.astype(BF16)) + f * sv_ref[c]
        den = w_inter * jnp.sum(qc.astype(F32) * n, axis=1, keepdims=True) + f * rs0
        hraw = num * (1.0 / jnp.maximum(jnp.abs(den), jnp.exp(-m_t)))
        y = _rms(hraw, nw_ref[...])
        out_ref[0, sl, :] = (y * _sigmoid(o_ref[0, sl, :])).astype(BF16)

        m_new = jnp.maximum(bL + m, gmax)
        a = jnp.exp(bL + m - m_new)
        fk = jnp.exp(gmax - m_new)
        C = a * C + fk * kv_ref[c]
        n = a * n + fk * ksum
        m = m_new


def _mlstm(zA, gates, gate_bias, conv_w, conv_b, norm_w, chunk=MLSTM_CHUNK):
    B, S, four_dm = zA.shape
    dm = four_dm // 4
    H = MLSTM_HEADS
    dh = dm // H
    kern = functools.partial(_mlstm_kernel, chunk=chunk)

    def col(base):
        return pl.BlockSpec((1, S, dh), lambda b, h: (b, 0, base + h))

    return pl.pallas_call(
        kern, grid=(B, H),
        in_specs=[col(0), col(H), col(2 * H), col(3 * H),
                  pl.BlockSpec((1, S, LANES), lambda b, h: (b, 0, 0)),
                  pl.BlockSpec((1, LANES), lambda b, h: (0, 0)),
                  pl.BlockSpec((MLSTM_CONV, dh), lambda b, h: (0, h)),
                  pl.BlockSpec((MLSTM_CONV, dh), lambda b, h: (0, H + h)),
                  pl.BlockSpec((1, dh), lambda b, h: (0, h)),
                  pl.BlockSpec((1, dh), lambda b, h: (0, H + h)),
                  pl.BlockSpec((None, 1, dh), lambda b, h: (h, 0, 0))],
        out_specs=pl.BlockSpec((1, S, dh), lambda b, h: (b, 0, h)),
        out_shape=jax.ShapeDtypeStruct((B, S, dm), BF16),
        scratch_shapes=[pltpu.VMEM((S + 8, dh), F32), pltpu.VMEM((S, dh), BF16), pltpu.VMEM((S, dh), BF16),
                        pltpu.VMEM((S // chunk, chunk, dh), F32), pltpu.VMEM((S // chunk, dh, dh), F32)],
        compiler_params=_params("parallel", "parallel"), name="mlstm")(
            zA, zA, zA, zA, gates, gate_bias, conv_w, conv_w, conv_b.reshape(1, -1), conv_b.reshape(1, -1),
            norm_w.reshape(H, 1, dh))


def _moba_kernel(rb_ref, q_ref, k_ref, v_ref, qkn_ref, out_ref,
                 bias_ref, sc_ref, qa_ref, ka_ref, va_ref):
    head = pl.program_id(0)
    S, dh = q_ref.shape[1], q_ref.shape[2]
    BLK = MOBA_BLOCK
    NB = S // BLK
    scale = dh ** -0.5

    @pl.when(pl.program_id(1) == 0)
    def _build_bias():
        rr = lax.broadcasted_iota(jnp.int32, (BLK, BLK), 0)
        cc = lax.broadcasted_iota(jnp.int32, (BLK, BLK), 1)
        for delta in range(NB):
            d = (rr - cc) + delta * BLK
            lo, hi = max(delta * BLK - (BLK - 1), 0), delta * BLK + BLK - 1
            tile = jnp.full((BLK, BLK), rb_ref[int(_T5_BUCKET[lo]), head], F32)
            for kb in range(int(_T5_BUCKET[lo]) + 1, int(_T5_BUCKET[hi]) + 1):
                tile = jnp.where(d >= _T5_THR[kb], rb_ref[kb, head], tile)
            if delta == 0:
                tile = jnp.where(d >= 0, tile, -jnp.inf)
            bias_ref[delta] = tile

    qn = _rms(q_ref[0], qkn_ref[0:1, :])
    kn = _rms(k_ref[0], qkn_ref[1:2, :])
    q_hi = qn.astype(BF16)
    qa_ref[:, :, :dh] = q_hi.reshape(NB, BLK, dh)
    lane_k = lax.broadcasted_iota(jnp.int32, (BLK, dh), 1)
    ka_ref[:, :, :dh] = kn.astype(BF16).reshape(NB, BLK, dh)
    for j in range(NB):
        ka_ref[j, :, dh:] = jnp.where(lane_k == j, 1.0, 0.0).astype(BF16)
    va_ref[:, :, :dh] = v_ref[0].astype(BF16).reshape(NB, BLK, dh)
    va_ref[:, :, dh:] = jnp.ones((NB, BLK, dh), BF16)

    kmean = jnp.mean(kn.reshape(NB, BLK, dh), axis=1)
    kmean = jnp.concatenate([kmean, jnp.zeros((16 - NB, dh), F32)], axis=0)
    q_lo = (qn - q_hi.astype(F32)).astype(BF16)
    m_hi = kmean.astype(BF16)
    m_lo = (kmean - m_hi.astype(F32)).astype(BF16)
    gate_t = _dot_nt(m_hi, q_hi) + (_dot_nt(m_lo, q_hi) + _dot_nt(m_hi, q_lo))

    blk = lax.broadcasted_iota(jnp.int32, (NB, BLK), 0)
    for i in range(1, NB):
        g = gate_t[:NB, i * BLK:(i + 1) * BLK]
        cnt = jnp.zeros((NB, BLK), F32)
        for jp in range(i):
            gj = g[jp:jp + 1, :]
            beats = (gj > g) | ((gj == g) & (blk > jp))
            cnt = cnt + jnp.where(beats, 1.0, 0.0)
        mask_t = jnp.where((cnt < MOBA_TOPK) | (blk >= i), 0.0, MASKED)
        mask_t = jnp.concatenate([mask_t, jnp.zeros((dh - NB, BLK), F32)], axis=0)
        qa_ref[i, :, dh:] = mask_t.T.astype(BF16)
    qa_ref[0, :, dh:] = jnp.zeros((BLK, dh), BF16)

    tile = lambda i, j: i * (i + 1) // 2 + j

    def scores(i):
        qa = qa_ref[i]
        mx = None
        for j in range(i + 1):
            s = _dot_nt(qa, ka_ref[j]) * scale + bias_ref[i - j]
            sc_ref[tile(i, j)] = s
            mj = jnp.maximum(s[:, :LANES], s[:, LANES:])
            mx = mj if mx is None else jnp.maximum(mx, mj)
        return jnp.max(mx, axis=1, keepdims=True)

    def attend(i, mx):
        acc = jnp.zeros((BLK, 2 * dh), F32)
        for j in range(i + 1):
            acc = acc + _dot(jnp.exp(sc_ref[tile(i, j)] - mx).astype(BF16), va_ref[j])
        out_ref[0, i * BLK:(i + 1) * BLK, :] = (acc[:, :dh] * (1.0 / acc[:, dh:dh + 1])).astype(BF16)

    mx_prev = scores(0)
    for i in range(1, NB):
        mx_next = scores(i)
        attend(i - 1, mx_prev)
        mx_prev = mx_next
    attend(NB - 1, mx_prev)


def _moba(zB, qk_norm_l, rel_bias):
    B, S, three_da = zB.shape
    da = three_da // 3
    H = MOBA_HEADS
    dh = da // H
    NB = S // MOBA_BLOCK
    BLK = MOBA_BLOCK

    def col(base):
        return pl.BlockSpec((1, S, dh), lambda h, b: (b, 0, base + h))

    return pl.pallas_call(
        _moba_kernel, grid=(H, B),
        in_specs=[pl.BlockSpec(memory_space=pltpu.SMEM),
                  col(0), col(H), col(2 * H),
                  pl.BlockSpec((2, dh), lambda h, b: (0, 0))],
        out_specs=pl.BlockSpec((1, S, dh), lambda h, b: (b, 0, h)),
        out_shape=jax.ShapeDtypeStruct((B, S, da), BF16),
        scratch_shapes=[pltpu.VMEM((NB, BLK, BLK), F32), pltpu.VMEM((NB * (NB + 1) // 2, BLK, BLK), F32),
                        pltpu.VMEM((NB, BLK, 2 * dh), BF16), pltpu.VMEM((NB, BLK, 2 * dh), BF16),
                        pltpu.VMEM((NB, BLK, 2 * dh), BF16)],
        compiler_params=_params("parallel", "arbitrary"), name="moba")(rel_bias, zB, zB, zB, qk_norm_l)


def kernel(x, norm_mix, w_in, gate_bias, conv_qk_w, conv_qk_b, mlstm_norm, qk_norm, rel_bias, w_out, norm_ffn,
           w_up, conv_ffn_w, conv_ffn_b, w_down):
    B, S, D = x.shape
    depth = w_in.shape[0]
    H = MLSTM_HEADS
    dm = mlstm_norm.shape[1] * mlstm_norm.shape[2]
    da = w_out.shape[1] - dm
    n_gate = 2 * H
    g0 = 4 * dm
    assert w_in.shape[2] == g0 + n_gate + 3 * da

    gb = jnp.pad(gate_bias, ((0, 0), (0, LANES - n_gate)))

    w_in_t = jnp.swapaxes(w_in, 1, 2)

    x2 = x.reshape(B * S, D)
    for l in range(depth):
        h, gates = _norm(x2, norm_mix[l], w_in_t, l, g0)
        zA = _proj_t(h, w_in_t, l, 0, g0)
        zB = _proj_t(h, w_in_t, l, g0 + n_gate, 3 * da)
        hm = _mlstm(zA.reshape(B, S, g0), gates.reshape(B, S, LANES), gb[l:l + 1], conv_qk_w[l], conv_qk_b[l],
                    mlstm_norm[l])
        ha = _moba(zB.reshape(B, S, 3 * da), qk_norm[l], rel_bias)
        x2 = _out_proj(hm.reshape(B * S, dm), ha.reshape(B * S, da), w_out, l, x2)
        h2 = _norm(x2, norm_ffn[l])
        a, w_down_bf = _up_proj(h2, w_up, l, conv_ffn_w[l], conv_ffn_b[l], w_down, S)
        x2 = _down_proj(a, w_down_bf, x2)
    return x2.reshape(B, S, D)
```

```python
import functools
import math

import numpy as np
import jax
import jax.numpy as jnp
from jax import lax
from jax.experimental import pallas as pl
from jax.experimental.pallas import tpu as pltpu

F32 = jnp.float32
BF16 = jnp.bfloat16

EPS = 1e-6
MLSTM_HEADS = 4
MLSTM_CONV = 4
MLSTM_CHUNK = 256
MOBA_HEADS = 8
MOBA_BLOCK = 256
MOBA_TOPK = 3
MASKED = -1e30
N_BUCKETS = 32
MAX_DISTANCE = 2048
FFN_CONV = 3
HALO = 8
UP_ROWS = 256
UP_COLS = 256

LANES = 128
VMEM_LIMIT = 56 * 1024 * 1024


def _t5_thresholds():
    max_exact = N_BUCKETS // 2
    d = np.arange(0, MAX_DISTANCE, dtype=np.float64)
    ratio = np.maximum(d, max_exact) / max_exact
    large = max_exact + (np.log(ratio) / math.log(MAX_DISTANCE / max_exact)
                         * (N_BUCKETS - max_exact)).astype(np.int64)
    bucket = np.where(d < max_exact, d.astype(np.int64), np.minimum(large, N_BUCKETS - 1))
    assert np.all(np.diff(bucket) >= 0)
    thr = [int(np.argmax(bucket >= k)) if np.any(bucket >= k) else MAX_DISTANCE for k in range(N_BUCKETS)]
    return bucket, thr


_T5_BUCKET, _T5_THR = _t5_thresholds()


def _sigmoid(x):
    return 1.0 / (1.0 + jnp.exp(-x))


def _split3(x):
    hi = x.astype(BF16)
    r1 = x - hi.astype(F32)
    mid = r1.astype(BF16)
    lo = (r1 - mid.astype(F32)).astype(BF16)
    return hi, mid, lo


def _dot(a, b):
    return jnp.dot(a, b, preferred_element_type=F32)


def _dot_nt(a, b):
    return lax.dot_general(a, b, (((1,), (1,)), ((), ())), preferred_element_type=F32)


def _dot_tn(a, b):
    return lax.dot_general(a, b, (((0,), (0,)), ((), ())), preferred_element_type=F32)


def _params(*sem, flags=None):
    return pltpu.CompilerParams(dimension_semantics=sem, vmem_limit_bytes=VMEM_LIMIT, flags=flags)


def _rms(x, g):
    return x * lax.rsqrt(jnp.mean(x * x, axis=-1, keepdims=True) + EPS) * g


def _norm_kernel(x_ref, g_ref, h_ref):
    h_ref[...] = _rms(x_ref[...], g_ref[...]).astype(BF16)


def _norm_gates_kernel(x_ref, g_ref, wg_ref, h_ref, gates_ref):
    h = _rms(x_ref[...], g_ref[...]).astype(BF16)
    h_ref[...] = h
    gates_ref[...] = _dot_nt(h, wg_ref[...].astype(BF16))


def _norm(x2, g, wt=None, layer=0, gate_row=None, tm=1024):
    T, D = x2.shape
    x_spec = pl.BlockSpec((tm, D), lambda i: (i, 0))
    g_spec = pl.BlockSpec((1, D), lambda i: (0, 0))
    h_spec = pl.BlockSpec((tm, D), lambda i: (i, 0))
    if wt is None:
        return pl.pallas_call(
            _norm_kernel, grid=(T // tm,), in_specs=[x_spec, g_spec], out_specs=h_spec,
            out_shape=jax.ShapeDtypeStruct((T, D), BF16), compiler_params=_params("parallel"),
            name="rmsnorm")(x2, g.reshape(1, D))
    assert gate_row % LANES == 0
    return pl.pallas_call(
        _norm_gates_kernel, grid=(T // tm,),
        in_specs=[x_spec, g_spec, pl.BlockSpec((None, LANES, D), lambda i: (layer, gate_row // LANES, 0))],
        out_specs=[h_spec, pl.BlockSpec((tm, LANES), lambda i: (i, 0))],
        out_shape=[jax.ShapeDtypeStruct((T, D), BF16), jax.ShapeDtypeStruct((T, LANES), F32)],
        compiler_params=_params("parallel"), name="rmsnorm_gates")(x2, g.reshape(1, D), wt)


def _proj_t_kernel(h_ref, w_ref, o_ref, wbf_ref):
    @pl.when(pl.program_id(1) == 0)
    def _():
        wbf_ref[...] = w_ref[...].astype(BF16)

    o_ref[...] = _dot_nt(h_ref[...], wbf_ref[...]).astype(o_ref.dtype)


def _proj_t_shifted_kernel(h_ref, w_ref, wnext_ref, o_ref, wbf_ref, *, shift):
    @pl.when(pl.program_id(1) == 0)
    def _():
        wbf_ref[...] = jnp.concatenate([w_ref[shift:, :], wnext_ref[...]], axis=0).astype(BF16)

    o_ref[...] = _dot_nt(h_ref[...], wbf_ref[...]).astype(o_ref.dtype)


def _proj_t(h, wt, layer, row0, nrows, tm=1024, tn=1024, out_dtype=F32):
    T, K = h.shape
    shift = row0 % tn
    base = row0 - shift
    assert nrows % tn == 0 and T % tm == 0
    r0 = base // tn
    h_spec = pl.BlockSpec((tm, K), lambda n, m: (m, 0))
    w_spec = pl.BlockSpec((None, tn, K), lambda n, m: (layer, r0 + n, 0))
    common = dict(
        grid=(nrows // tn, T // tm),
        out_specs=pl.BlockSpec((tm, tn), lambda n, m: (m, n)),
        out_shape=jax.ShapeDtypeStruct((T, nrows), out_dtype),
        scratch_shapes=[pltpu.VMEM((tn, K), BF16)],
        compiler_params=_params("parallel", "arbitrary"), name="in_proj")
    if shift == 0:
        return pl.pallas_call(_proj_t_kernel, in_specs=[h_spec, w_spec], **common)(h, wt)
    assert shift % 16 == 0 or shift == 8, shift
    assert tn % shift == 0 and base % shift == 0
    per_tile = tn // shift
    wnext_spec = pl.BlockSpec((None, shift, K), lambda n, m: (layer, (r0 + n + 1) * per_tile, 0))
    return pl.pallas_call(functools.partial(_proj_t_shifted_kernel, shift=shift),
                          in_specs=[h_spec, w_spec, wnext_spec], **common)(h, wt, wt)


def _out_proj_kernel(hm_ref, ha_ref, w_ref, x_ref, o_ref, wbf_ref):
    @pl.when(pl.program_id(1) == 0)
    def _():
        wbf_ref[...] = w_ref[...].astype(BF16)

    km = hm_ref.shape[1]
    o_ref[...] = x_ref[...] + (_dot(hm_ref[...], wbf_ref[:km, :]) + _dot(ha_ref[...], wbf_ref[km:, :]))


def _out_proj(hm, ha, w, layer, x2, tm=1024, tn=1024):
    T, km = hm.shape
    ka = ha.shape[1]
    N = x2.shape[1]
    return pl.pallas_call(
        _out_proj_kernel, grid=(N // tn, T // tm),
        in_specs=[pl.BlockSpec((tm, km), lambda n, m: (m, 0)),
                  pl.BlockSpec((tm, ka), lambda n, m: (m, 0)),
                  pl.BlockSpec((None, km + ka, tn), lambda n, m: (layer, 0, n)),
                  pl.BlockSpec((tm, tn), lambda n, m: (m, n))],
        out_specs=pl.BlockSpec((tm, tn), lambda n, m: (m, n)),
        out_shape=jax.ShapeDtypeStruct((T, N), F32),
        scratch_shapes=[pltpu.VMEM((km + ka, tn), BF16)],
        compiler_params=_params("parallel", "arbitrary"), name="out_proj")(hm, ha, w, x2)


def _out_proj_norm_kernel(hm_ref, ha_ref, w_ref, x_ref, g_ref, o_ref, h_ref, wbf_ref):
    @pl.when(pl.program_id(0) == 0)
    def _():
        wbf_ref[...] = w_ref[...].astype(BF16)

    km = hm_ref.shape[1]
    x1 = x_ref[...] + (_dot(hm_ref[...], wbf_ref[:km, :]) + _dot(ha_ref[...], wbf_ref[km:, :]))
    o_ref[...] = x1
    h_ref[...] = _rms(x1, g_ref[...]).astype(BF16)


def _out_proj_norm(hm, ha, w, layer, x2, g, tm=512):
    T, km = hm.shape
    ka = ha.shape[1]
    N = x2.shape[1]
    return pl.pallas_call(
        _out_proj_norm_kernel, grid=(T // tm,),
        in_specs=[pl.BlockSpec((tm, km), lambda m: (m, 0)),
                  pl.BlockSpec((tm, ka), lambda m: (m, 0)),
                  pl.BlockSpec((None, km + ka, N), lambda m: (layer, 0, 0), pipeline_mode=pl.Buffered(1)),
                  pl.BlockSpec((tm, N), lambda m: (m, 0)),
                  pl.BlockSpec((1, N), lambda m: (0, 0))],
        out_specs=[pl.BlockSpec((tm, N), lambda m: (m, 0)), pl.BlockSpec((tm, N), lambda m: (m, 0))],
        out_shape=[jax.ShapeDtypeStruct((T, N), F32), jax.ShapeDtypeStruct((T, N), BF16)],
        scratch_shapes=[pltpu.VMEM((km + ka, N), BF16)],
        compiler_params=_params("arbitrary"), name="out_proj_norm")(hm, ha, w, x2, g.reshape(1, N))


def _down_proj_kernel(a_ref, w_ref, x_ref, o_ref):
    o_ref[...] = x_ref[...] + _dot(a_ref[...], w_ref[...])


def _down_proj(a, w_bf, x2, tm=512, tn=1024):
    T, K = a.shape
    N = x2.shape[1]
    return pl.pallas_call(
        _down_proj_kernel, grid=(N // tn, T // tm),
        in_specs=[pl.BlockSpec((tm, K), lambda n, m: (m, 0)),
                  pl.BlockSpec((K, tn), lambda n, m: (0, n)),
                  pl.BlockSpec((tm, tn), lambda n, m: (m, n))],
        out_specs=pl.BlockSpec((tm, tn), lambda n, m: (m, n)),
        out_shape=jax.ShapeDtypeStruct((T, N), F32),
        compiler_params=_params("parallel", "arbitrary"), name="down_proj")(a, w_bf, x2)


def _up_proj_kernel(h_ref, wg_ref, wv_ref, cwg_ref, cwv_ref, cbg_ref, cbv_ref, wd_ref, a_ref, wdbf_ref,
                    wgbf_ref, wvbf_ref, carry_g_ref, carry_v_ref, *, tiles_per_seq):
    m = pl.program_id(1)
    wdbf_ref[...] = wd_ref[...].astype(BF16)

    @pl.when(m == 0)
    def _():
        wgbf_ref[...] = wg_ref[...].astype(BF16)
        wvbf_ref[...] = wv_ref[...].astype(BF16)

    @pl.when(m % tiles_per_seq == 0)
    def _():
        carry_g_ref[...] = jnp.zeros_like(carry_g_ref)
        carry_v_ref[...] = jnp.zeros_like(carry_v_ref)

    tm, tf = a_ref.shape
    row8 = lax.broadcasted_iota(jnp.int32, (HALO, 1), 0)
    cwg, cwv, cbg, cbv = cwg_ref[...], cwv_ref[...], cbg_ref[...], cbv_ref[...]

    def shifted(u, tail, k):
        r = pltpu.roll(u, k, 0)
        top = jnp.where(row8 < k, pltpu.roll(tail, k, 0), r[:HALO])
        return jnp.concatenate([top, r[HALO:]], axis=0)

    def conv(u, tail, cw, cb):
        return ((cb + cw[0:1, :] * shifted(u, tail, 2)) + cw[1:2, :] * shifted(u, tail, 1)) + cw[2:3, :] * u

    def dot_cols(hb, w_ref):
        return jnp.concatenate([_dot(hb, w_ref[:, c0:c0 + UP_COLS]) for c0 in range(0, tf, UP_COLS)], axis=1)

    tail_g, tail_v = carry_g_ref[...], carry_v_ref[...]
    for r0 in range(0, tm, UP_ROWS):
        hb = h_ref[r0:r0 + UP_ROWS, :]
        ug = dot_cols(hb, wgbf_ref)
        uv = dot_cols(hb, wvbf_ref)
        yg = conv(ug, tail_g, cwg, cbg)
        yv = conv(uv, tail_v, cwv, cbv)
        a_ref[r0:r0 + UP_ROWS, :] = (yg * _sigmoid(yg) * yv).astype(BF16)
        tail_g, tail_v = ug[UP_ROWS - HALO:, :], uv[UP_ROWS - HALO:, :]
    carry_g_ref[...] = tail_g
    carry_v_ref[...] = tail_v


def _up_proj(h, w, layer, cw, cb, w_down, seq_len, tm=1024, tf=512):
    T, K = h.shape
    d_ff = w.shape[2] // 2
    nf = d_ff // tf
    nm = T // tm
    assert d_ff % tf == 0 and seq_len % tm == 0
    d_out = w_down.shape[2]
    slab = d_ff // (nf * nm)
    assert slab * nf * nm == d_ff and slab % 16 == 0
    kern = functools.partial(_up_proj_kernel, tiles_per_seq=seq_len // tm)
    return pl.pallas_call(
        kern, grid=(nf, nm),
        in_specs=[pl.BlockSpec((tm, K), lambda n, m: (m, 0)),
                  pl.BlockSpec((None, K, tf), lambda n, m: (layer, 0, n)),
                  pl.BlockSpec((None, K, tf), lambda n, m: (layer, 0, nf + n)),
                  pl.BlockSpec((FFN_CONV, tf), lambda n, m: (0, n)),
                  pl.BlockSpec((FFN_CONV, tf), lambda n, m: (0, nf + n)),
                  pl.BlockSpec((1, tf), lambda n, m: (0, n)),
                  pl.BlockSpec((1, tf), lambda n, m: (0, nf + n)),
                  pl.BlockSpec((None, slab, d_out), lambda n, m: (layer, n * nm + m, 0))],
        out_specs=[pl.BlockSpec((tm, tf), lambda n, m: (m, n)),
                   pl.BlockSpec((slab, d_out), lambda n, m: (n * nm + m, 0))],
        out_shape=[jax.ShapeDtypeStruct((T, d_ff), BF16), jax.ShapeDtypeStruct((d_ff, d_out), BF16)],
        scratch_shapes=[pltpu.VMEM((K, tf), BF16), pltpu.VMEM((K, tf), BF16),
                        pltpu.VMEM((HALO, tf), F32), pltpu.VMEM((HALO, tf), F32)],
        compiler_params=_params("parallel", "arbitrary"), name="up_proj")(
            h, w, w, cw, cw, cb.reshape(1, -1), cb.reshape(1, -1), w_down)


def _mlstm_kernel(q_ref, k_ref, v_ref, o_ref, g_ref, gb_ref, cwq_ref, cwk_ref, cbq_ref, cbk_ref, nw_ref,
                  out_ref, pad_ref, qs_ref, ks_ref, sv_ref, kv_ref, *, chunk):
    head = pl.program_id(1)
    S, dh = q_ref.shape[1], q_ref.shape[2]
    L = chunk
    nc = S // L
    halo = 8

    def conv_silu(x_ref, cw_ref, cb_ref, dst_ref, scale):
        pad_ref[0:halo, :] = jnp.zeros((halo, dh), F32)
        pad_ref[halo:halo + S, :] = x_ref[0]
        cw = cw_ref[...]
        cb = cb_ref[...]
        for c in range(nc):
            y = cb
            for j in range(MLSTM_CONV):
                off = halo + c * L - (MLSTM_CONV - 1) + j
                y = y + cw[j:j + 1, :] * pad_ref[off:off + L, :]
            dst_ref[c * L:(c + 1) * L, :] = (y * _sigmoid(y) * scale).astype(BF16)

    conv_silu(q_ref, cwq_ref, cbq_ref, qs_ref, dh ** -0.5)
    conv_silu(k_ref, cwk_ref, cbk_ref, ks_ref, 1.0)

    rr = lax.broadcasted_iota(jnp.int32, (L, L), 0)
    cc = lax.broadcasted_iota(jnp.int32, (L, L), 1)
    tri = rr >= cc
    eye = rr == cc
    tri_bf = jnp.where(tri, 1.0, 0.0).astype(BF16)
    lane = lax.broadcasted_iota(jnp.int32, (L, LANES), 1)

    def lane_pick(x, idx):
        return jnp.sum(jnp.where(lane == idx, x, 0.0), axis=1, keepdims=True)

    def local(c):
        sl = slice(c * L, (c + 1) * L)
        G = g_ref[0, sl, :] + gb_ref[...]
        li = lane_pick(G, head)
        LF = jnp.minimum(G, 0.0) - jnp.log1p(jnp.exp(-jnp.abs(G)))
        hi, mid, lo = _split3(LF)
        b = lane_pick(_dot(tri_bf, hi) + _dot(tri_bf, mid) + _dot(tri_bf, lo), MLSTM_HEADS + head)
        r_row = jnp.sum(jnp.where(eye, li - b, 0.0), axis=0, keepdims=True)
        Dm = jnp.where(tri, b + r_row, -jnp.inf)
        mrow = jnp.max(Dm, axis=1, keepdims=True)
        qc, kc, vc = qs_ref[sl, :], ks_ref[sl, :], v_ref[0, sl, :]
        s0 = _dot_nt(qc, kc) * jnp.exp(Dm - mrow)
        sv_ref[c] = _dot(s0.astype(BF16), vc.astype(BF16))
        bL = b[L - 1:L, :]
        gcol = bL - b + li
        gmax = jnp.max(gcol, axis=0, keepdims=True)
        wk0 = jnp.exp(gcol - gmax)
        kv_ref[c] = _dot_tn(kc, (wk0 * vc).astype(BF16))
        ksum = jnp.sum(wk0 * kc.astype(F32), axis=0, keepdims=True)
        return b, mrow, jnp.sum(s0, axis=1, keepdims=True), bL, gmax, ksum

    C = jnp.zeros((dh, dh), F32)
    n = jnp.zeros((1, dh), F32)
    m = jnp.zeros((1, 1), F32)
    for c in range(nc):
        sl = slice(c * L, (c + 1) * L)
        b, mrow, rs0, bL, gmax, ksum = local(c)
        inter = b + m
        m_t = jnp.maximum(inter, mrow)
        w_inter = jnp.exp(inter - m_t)
        f = jnp.exp(mrow - m_t)
        qc = qs_ref[sl, :]
        num = w_inter * _dot(qc, C.astype(BF16)) + f * sv_ref[c]
        den = w_inter * jnp.sum(qc.astype(F32) * n, axis=1, keepdims=True) + f * rs0
        hraw = num * (1.0 / jnp.maximum(jnp.abs(den), jnp.exp(-m_t)))
        y = _rms(hraw, nw_ref[...])
        out_ref[0, sl, :] = (y * _sigmoid(o_ref[0, sl, :])).astype(BF16)

        m_new = jnp.maximum(bL + m, gmax)
        a = jnp.exp(bL + m - m_new)
        fk = jnp.exp(gmax - m_new)
        C = a * C + fk * kv_ref[c]
        n = a * n + fk * ksum
        m = m_new


def _mlstm(zA, gates, gate_bias, conv_w, conv_b, norm_w, chunk=MLSTM_CHUNK):
    B, S, four_dm = zA.shape
    dm = four_dm // 4
    H = MLSTM_HEADS
    dh = dm // H
    kern = functools.partial(_mlstm_kernel, chunk=chunk)

    def col(base):
        return pl.BlockSpec((1, S, dh), lambda b, h: (b, 0, base + h))

    return pl.pallas_call(
        kern, grid=(B, H),
        in_specs=[col(0), col(H), col(2 * H), col(3 * H),
                  pl.BlockSpec((1, S, LANES), lambda b, h: (b, 0, 0)),
                  pl.BlockSpec((1, LANES), lambda b, h: (0, 0)),
                  pl.BlockSpec((MLSTM_CONV, dh), lambda b, h: (0, h)),
                  pl.BlockSpec((MLSTM_CONV, dh), lambda b, h: (0, H + h)),
                  pl.BlockSpec((1, dh), lambda b, h: (0, h)),
                  pl.BlockSpec((1, dh), lambda b, h: (0, H + h)),
                  pl.BlockSpec((None, 1, dh), lambda b, h: (h, 0, 0))],
        out_specs=pl.BlockSpec((1, S, dh), lambda b, h: (b, 0, h)),
        out_shape=jax.ShapeDtypeStruct((B, S, dm), BF16),
        scratch_shapes=[pltpu.VMEM((S + 8, dh), F32), pltpu.VMEM((S, dh), BF16), pltpu.VMEM((S, dh), BF16),
                        pltpu.VMEM((S // chunk, chunk, dh), F32), pltpu.VMEM((S // chunk, dh, dh), F32)],
        compiler_params=_params("parallel", "parallel"), name="mlstm")(
            zA, zA, zA, zA, gates, gate_bias, conv_w, conv_w, conv_b.reshape(1, -1), conv_b.reshape(1, -1),
            norm_w.reshape(H, 1, dh))


def _moba_kernel(rb_ref, q_ref, k_ref, v_ref, qkn_ref, out_ref,
                 bias_ref, sc_ref, qa_ref, ka_ref, va_ref):
    head = pl.program_id(0)
    S, dh = q_ref.shape[1], q_ref.shape[2]
    BLK = MOBA_BLOCK
    NB = S // BLK
    scale = dh ** -0.5

    @pl.when(pl.program_id(1) == 0)
    def _build_bias():
        rr = lax.broadcasted_iota(jnp.int32, (BLK, BLK), 0)
        cc = lax.broadcasted_iota(jnp.int32, (BLK, BLK), 1)
        for delta in range(NB):
            d = (rr - cc) + delta * BLK
            lo, hi = max(delta * BLK - (BLK - 1), 0), delta * BLK + BLK - 1
            tile = jnp.full((BLK, BLK), rb_ref[int(_T5_BUCKET[lo]), head], F32)
            for kb in range(int(_T5_BUCKET[lo]) + 1, int(_T5_BUCKET[hi]) + 1):
                tile = jnp.where(d >= _T5_THR[kb], rb_ref[kb, head], tile)
            if delta == 0:
                tile = jnp.where(d >= 0, tile, -jnp.inf)
            bias_ref[delta] = tile

    qn = _rms(q_ref[0], qkn_ref[0:1, :])
    kn = _rms(k_ref[0], qkn_ref[1:2, :])
    q_hi = qn.astype(BF16)
    qa_ref[:, :, :dh] = q_hi.reshape(NB, BLK, dh)
    lane_k = lax.broadcasted_iota(jnp.int32, (BLK, dh), 1)
    ka_ref[:, :, :dh] = kn.astype(BF16).reshape(NB, BLK, dh)
    for j in range(NB):
        ka_ref[j, :, dh:] = jnp.where(lane_k == j, 1.0, 0.0).astype(BF16)
    va_ref[:, :, :dh] = v_ref[0].astype(BF16).reshape(NB, BLK, dh)
    va_ref[:, :, dh:] = jnp.ones((NB, BLK, dh), BF16)

    kmean = jnp.mean(kn.reshape(NB, BLK, dh), axis=1)
    kmean = jnp.concatenate([kmean, jnp.zeros((16 - NB, dh), F32)], axis=0)
    q_lo = (qn - q_hi.astype(F32)).astype(BF16)
    m_hi = kmean.astype(BF16)
    m_lo = (kmean - m_hi.astype(F32)).astype(BF16)
    gate_t = _dot_nt(m_hi, q_hi) + (_dot_nt(m_lo, q_hi) + _dot_nt(m_hi, q_lo))

    blk = lax.broadcasted_iota(jnp.int32, (NB, BLK), 0)
    for i in range(1, NB):
        g = gate_t[:NB, i * BLK:(i + 1) * BLK]
        cnt = jnp.zeros((NB, BLK), F32)
        for jp in range(i):
            gj = g[jp:jp + 1, :]
            beats = (gj > g) | ((gj == g) & (blk > jp))
            cnt = cnt + jnp.where(beats, 1.0, 0.0)
        mask_t = jnp.where((cnt < MOBA_TOPK) | (blk >= i), 0.0, MASKED)
        mask_t = jnp.concatenate([mask_t, jnp.zeros((dh - NB, BLK), F32)], axis=0)
        qa_ref[i, :, dh:] = mask_t.T.astype(BF16)
    qa_ref[0, :, dh:] = jnp.zeros((BLK, dh), BF16)

    tile = lambda i, j: i * (i + 1) // 2 + j

    def scores(i):
        qa = qa_ref[i]
        mx = None
        for j in range(i + 1):
            s = _dot_nt(qa, ka_ref[j]) * scale + bias_ref[i - j]
            sc_ref[tile(i, j)] = s
            mj = jnp.maximum(s[:, :LANES], s[:, LANES:])
            mx = mj if mx is None else jnp.maximum(mx, mj)
        return jnp.max(mx, axis=1, keepdims=True)

    def attend(i, mx):
        acc = jnp.zeros((BLK, 2 * dh), F32)
        for j in range(i + 1):
            acc = acc + _dot(jnp.exp(sc_ref[tile(i, j)] - mx).astype(BF16), va_ref[j])
        out_ref[0, i * BLK:(i + 1) * BLK, :] = (acc[:, :dh] * (1.0 / acc[:, dh:dh + 1])).astype(BF16)

    mx_prev = scores(0)
    for i in range(1, NB):
        mx_next = scores(i)
        attend(i - 1, mx_prev)
        mx_prev = mx_next
    attend(NB - 1, mx_prev)


def _moba(zB, qk_norm_l, rel_bias):
    B, S, three_da = zB.shape
    da = three_da // 3
    H = MOBA_HEADS
    dh = da // H
    NB = S // MOBA_BLOCK
    BLK = MOBA_BLOCK

    def col(base):
        return pl.BlockSpec((1, S, dh), lambda h, b: (b, 0, base + h))

    return pl.pallas_call(
        _moba_kernel, grid=(H, B),
        in_specs=[pl.BlockSpec(memory_space=pltpu.SMEM),
                  col(0), col(H), col(2 * H),
                  pl.BlockSpec((2, dh), lambda h, b: (0, 0))],
        out_specs=pl.BlockSpec((1, S, dh), lambda h, b: (b, 0, h)),
        out_shape=jax.ShapeDtypeStruct((B, S, da), BF16),
        scratch_shapes=[pltpu.VMEM((NB, BLK, BLK), F32), pltpu.VMEM((NB * (NB + 1) // 2, BLK, BLK), F32),
                        pltpu.VMEM((NB, BLK, 2 * dh), BF16), pltpu.VMEM((NB, BLK, 2 * dh), BF16),
                        pltpu.VMEM((NB, BLK, 2 * dh), BF16)],
        compiler_params=_params("parallel", "arbitrary"), name="moba")(rel_bias, zB, zB, zB, qk_norm_l)


def kernel(x, norm_mix, w_in, gate_bias, conv_qk_w, conv_qk_b, mlstm_norm, qk_norm, rel_bias, w_out, norm_ffn,
           w_up, conv_ffn_w, conv_ffn_b, w_down):
    B, S, D = x.shape
    depth = w_in.shape[0]
    H = MLSTM_HEADS
    dm = mlstm_norm.shape[1] * mlstm_norm.shape[2]
    da = w_out.shape[1] - dm
    n_gate = 2 * H
    g0 = 4 * dm
    assert w_in.shape[2] == g0 + n_gate + 3 * da

    gb = jnp.pad(gate_bias, ((0, 0), (0, LANES - n_gate)))

    w_in_t = jnp.swapaxes(w_in, 1, 2)

    x2 = x.reshape(B * S, D)
    for l in range(depth):
        h, gates = _norm(x2, norm_mix[l], w_in_t, l, g0)
        zA = _proj_t(h, w_in_t, l, 0, g0)
        zB = _proj_t(h, w_in_t, l, g0 + n_gate, 3 * da)
        hm = _mlstm(zA.reshape(B, S, g0), gates.reshape(B, S, LANES), gb[l:l + 1], conv_qk_w[l], conv_qk_b[l],
                    mlstm_norm[l])
        ha = _moba(zB.reshape(B, S, 3 * da), qk_norm[l], rel_bias)
        x2, h2 = _out_proj_norm(hm.reshape(B * S, dm), ha.reshape(B * S, da), w_out, l, x2, norm_ffn[l])
        a, w_down_bf = _up_proj(h2, w_up, l, conv_ffn_w[l], conv_ffn_b[l], w_down, S)
        x2 = _down_proj(a, w_down_bf, x2)
    return x2.reshape(B, S, D)
```
